```python
import jax
import jax.numpy as jnp
from jax import lax
import numpy as np

D_MODEL = 1024
BATCH = 8
SEQ = 2048
DEPTH = 1
DEC_BATCH = 128
DEC_SEQ = 8
PAST_LEN = 16384
PAGE_SIZE = 128

N_HEADS = 8
HEAD_K = D_MODEL // N_HEADS
HEAD_V = D_MODEL // N_HEADS
QK_WIDTH = N_HEADS * HEAD_K
V_WIDTH = N_HEADS * HEAD_V
QKV_WIDTH = 2 * QK_WIDTH + V_WIDTH
SHORT_CONV = 4
CHUNK = 64
CONF_WIDTH = D_MODEL
CONF_KERNEL = 31
D_FF = 4 * D_MODEL
EPS = 1e-6
IN_WIDTH = QKV_WIDTH + 2 * N_HEADS + V_WIDTH + 2 * CONF_WIDTH + 2 * D_MODEL

kernel_name = 'hybrid_gdn_conformer_step'


def rmsnorm(x, g):
    xf = x.astype(jnp.float32)
    y = xf * lax.rsqrt(jnp.mean(xf * xf, axis=-1, keepdims=True) + EPS)
    return (y * g.astype(jnp.float32)).astype(x.dtype)


def layernorm(x, g, b):
    xf = x.astype(jnp.float32)
    mu = jnp.mean(xf, axis=-1, keepdims=True)
    xc = xf - mu
    var = jnp.mean(xc * xc, axis=-1, keepdims=True)
    y = xc * lax.rsqrt(var + EPS) * g.astype(jnp.float32) + b.astype(jnp.float32)
    return y.astype(x.dtype)


def l2norm(x):
    xf = x.astype(jnp.float32)
    return xf * lax.rsqrt(jnp.sum(xf * xf, axis=-1, keepdims=True) + EPS)


def causal_dwconv(x_ext, w):
    c = x_ext.shape[-1]
    return lax.conv_general_dilated(
        x_ext, w[:, None, :].astype(x_ext.dtype), window_strides=(1,), padding='VALID',
        dimension_numbers=('NWC', 'WIO', 'NWC'), feature_group_count=c)


def gated_delta_chunked(q, k, v, g, beta, s0):
    B, L, H, DK = q.shape
    DV = v.shape[-1]
    n = -(-L // CHUNK)
    pad = n * CHUNK - L

    def blocks(t):
        t = jnp.pad(t, [(0, 0), (0, pad)] + [(0, 0)] * (t.ndim - 2))
        t = t.reshape((B, n, CHUNK) + t.shape[2:])
        return jnp.moveaxis(t, 3, 2)

    q, k, v, g, beta = blocks(q), blocks(k), blocks(v), blocks(g), blocks(beta)
    gam = jnp.cumsum(g, axis=-1)
    idx = jnp.arange(CHUNK)
    causal = idx[:, None] >= idx[None, :]
    strict = idx[:, None] > idx[None, :]
    decay = jnp.exp(jnp.where(causal, gam[..., :, None] - gam[..., None, :], -jnp.inf))
    a = jnp.where(strict, beta[..., :, None] * jnp.einsum('bnhid,bnhjd->bnhij', k, k) * decay, 0.0)
    lhs = a + jnp.eye(CHUNK, dtype=a.dtype)
    rhs = jnp.concatenate([beta[..., None] * v, (beta * jnp.exp(gam))[..., None] * k], axis=-1)
    sol = lax.linalg.triangular_solve(lhs, rhs, left_side=True, lower=True, unit_diagonal=True)
    vw, kd = sol[..., :DV], sol[..., DV:]
    qk = jnp.einsum('bnhid,bnhjd->bnhij', q, k) * decay
    qd = q * jnp.exp(gam)[..., None]
    ko = k * jnp.exp(gam[..., -1:] - gam)[..., None]
    gl = jnp.exp(gam[..., -1])
    xs = tuple(jnp.moveaxis(t, 1, 0) for t in (vw, kd, qk, qd, ko, gl))

    def step(s, c):
        vw_c, kd_c, qk_c, qd_c, ko_c, gl_c = c
        u = vw_c - jnp.einsum('bhcd,bhde->bhce', kd_c, s)
        o = jnp.einsum('bhcd,bhde->bhce', qd_c, s) + jnp.einsum('bhij,bhje->bhie', qk_c, u)
        s = gl_c[..., None, None] * s + jnp.einsum('bhcd,bhce->bhde', ko_c, u)
        return s, o

    s_fin, o = lax.scan(step, s0, xs)
    o = jnp.transpose(o, (1, 0, 3, 2, 4)).reshape(B, n * CHUNK, H, DV)[:, :L]
    return o, s_fin


def hybrid_layer(x, s_delta, buf_qkv, buf_conf, w_in, conv_qkv_w, a_log, dt_bias, delta_norm_g,
                 conf_dw_w, conf_dw_b, conf_ln_g, conf_ln_b, w_out,
                 g_pre_mix, g_post_mix, g_pre_ffn, g_post_ffn, w_up, w_down):
    B, L, _ = x.shape
    f32 = jnp.float32
    h = rmsnorm(x, g_pre_mix)
    proj = jnp.einsum('bld,de->ble', h, w_in)
    o0 = QKV_WIDTH
    o1 = o0 + N_HEADS
    o2 = o1 + N_HEADS
    o3 = o2 + V_WIDTH
    o4 = o3 + 2 * CONF_WIDTH
    qkv_pre = proj[..., :o0]
    b_logit = proj[..., o0:o1]
    a_logit = proj[..., o1:o2]
    z = proj[..., o2:o3]
    glu_in = proj[..., o3:o4]
    gate_a = proj[..., o4:o4 + D_MODEL]
    gate_b = proj[..., o4 + D_MODEL:]

    qkv_ext = jnp.concatenate([buf_qkv.astype(x.dtype), qkv_pre], axis=1)
    qkv = jax.nn.silu(causal_dwconv(qkv_ext, conv_qkv_w))
    q = l2norm(qkv[..., :QK_WIDTH].reshape(B, L, N_HEADS, HEAD_K)) * (HEAD_K ** -0.5)
    k = l2norm(qkv[..., QK_WIDTH:2 * QK_WIDTH].reshape(B, L, N_HEADS, HEAD_K))
    v = qkv[..., 2 * QK_WIDTH:].reshape(B, L, N_HEADS, HEAD_V).astype(f32)
    beta = jax.nn.sigmoid(b_logit.astype(f32))
    g = -jnp.exp(a_log.astype(f32)) * jax.nn.softplus(a_logit.astype(f32) + dt_bias.astype(f32))
    o, s_new = gated_delta_chunked(q, k, v, g, beta, s_delta.astype(f32))
    o = o * lax.rsqrt(jnp.mean(o * o, axis=-1, keepdims=True) + EPS) * delta_norm_g.astype(f32)
    o = o * jax.nn.silu(z.astype(f32).reshape(B, L, N_HEADS, HEAD_V))
    o_a = o.reshape(B, L, V_WIDTH).astype(x.dtype)

    glu = glu_in[..., :CONF_WIDTH] * jax.nn.sigmoid(glu_in[..., CONF_WIDTH:])
    conf_ext = jnp.concatenate([buf_conf.astype(x.dtype), glu], axis=1)
    c = causal_dwconv(conf_ext, conf_dw_w) + conf_dw_b.astype(x.dtype)
    o_b = jax.nn.silu(layernorm(c, conf_ln_g, conf_ln_b))

    mixed = jax.nn.sigmoid(gate_a) * o_a + jax.nn.sigmoid(gate_b) * o_b
    x = x + rmsnorm(jnp.einsum('ble,ed->bld', mixed, w_out), g_post_mix)

    hf = jnp.einsum('bld,df->blf', rmsnorm(x, g_pre_ffn), w_up)
    f = jnp.square(jax.nn.relu(hf))
    y = x + rmsnorm(jnp.einsum('blf,fd->bld', f, w_down), g_post_ffn)
    return (y, s_new.astype(s_delta.dtype),
            qkv_ext[:, -(SHORT_CONV - 1):], conf_ext[:, -(CONF_KERNEL - 1):])


def setup_inputs(seed: int = 0) -> dict:
    key = jax.random.key(seed)
    ks = jax.random.split(key, 24)
    nrm = jax.random.normal
    x_prompt = nrm(ks[0], (BATCH, SEQ, D_MODEL), jnp.float32)
    x_sample = nrm(ks[1], (DEC_BATCH, DEC_SEQ, D_MODEL), jnp.float32)
    state_delta = nrm(ks[2], (DEPTH, DEC_BATCH, N_HEADS, HEAD_K, HEAD_V), jnp.float32) * HEAD_K ** -0.5
    state_qkv_conv = nrm(ks[3], (DEPTH, DEC_BATCH, SHORT_CONV - 1, QKV_WIDTH), jnp.float32)
    state_conf_conv = nrm(ks[4], (DEPTH, DEC_BATCH, CONF_KERNEL - 1, CONF_WIDTH), jnp.float32)
    w_in = nrm(ks[5], (DEPTH, D_MODEL, IN_WIDTH), jnp.float32) * D_MODEL ** -0.5
    conv_qkv_w = nrm(ks[6], (DEPTH, SHORT_CONV, QKV_WIDTH), jnp.float32) * SHORT_CONV ** -0.5
    a_log = jnp.log(jax.random.uniform(ks[7], (DEPTH, N_HEADS), jnp.float32, 1.0, 16.0))
    dt = jnp.exp(jax.random.uniform(ks[8], (DEPTH, N_HEADS), jnp.float32, np.log(1e-3), np.log(1e-1)))
    dt_bias = dt + jnp.log(-jnp.expm1(-dt))
    delta_norm_g = 1.0 + 0.1 * nrm(ks[9], (DEPTH, HEAD_V), jnp.float32)
    conf_dw_w = nrm(ks[10], (DEPTH, CONF_KERNEL, CONF_WIDTH), jnp.float32) * CONF_KERNEL ** -0.5
    conf_dw_b = 0.02 * nrm(ks[11], (DEPTH, CONF_WIDTH), jnp.float32)
    conf_ln_g = 1.0 + 0.1 * nrm(ks[12], (DEPTH, CONF_WIDTH), jnp.float32)
    conf_ln_b = 0.02 * nrm(ks[13], (DEPTH, CONF_WIDTH), jnp.float32)
    w_out = nrm(ks[14], (DEPTH, D_MODEL, D_MODEL), jnp.float32) * D_MODEL ** -0.5
    g_pre_mix = 1.0 + 0.1 * nrm(ks[15], (DEPTH, D_MODEL), jnp.float32)
    g_post_mix = 1.0 + 0.1 * nrm(ks[16], (DEPTH, D_MODEL), jnp.float32)
    g_pre_ffn = 1.0 + 0.1 * nrm(ks[17], (DEPTH, D_MODEL), jnp.float32)
    g_post_ffn = 1.0 + 0.1 * nrm(ks[18], (DEPTH, D_MODEL), jnp.float32)
    w_up = nrm(ks[19], (DEPTH, D_MODEL, D_FF), jnp.float32) * D_MODEL ** -0.5
    w_down = nrm(ks[20], (DEPTH, D_FF, D_MODEL), jnp.float32) * D_FF ** -0.5
    return {'x_prompt': x_prompt, 'x_sample': x_sample,
            'state_delta': state_delta, 'state_qkv_conv': state_qkv_conv,
            'state_conf_conv': state_conf_conv,
            'w_in': w_in, 'conv_qkv_w': conv_qkv_w, 'a_log': a_log, 'dt_bias': dt_bias,
            'delta_norm_g': delta_norm_g, 'conf_dw_w': conf_dw_w, 'conf_dw_b': conf_dw_b,
            'conf_ln_g': conf_ln_g, 'conf_ln_b': conf_ln_b, 'w_out': w_out,
            'g_pre_mix': g_pre_mix, 'g_post_mix': g_post_mix, 'g_pre_ffn': g_pre_ffn,
            'g_post_ffn': g_post_ffn, 'w_up': w_up, 'w_down': w_down}


def reference(x_prompt, x_sample, state_delta, state_qkv_conv, state_conf_conv,
              w_in, conv_qkv_w, a_log, dt_bias, delta_norm_g, conf_dw_w, conf_dw_b,
              conf_ln_g, conf_ln_b, w_out, g_pre_mix, g_post_mix, g_pre_ffn, g_post_ffn,
              w_up, w_down):
    B = x_prompt.shape[0]
    xp, xs = x_prompt, x_sample
    dp, qp, cp, ds, qs, cs = [], [], [], [], [], []
    for l in range(DEPTH):
        lw = (w_in[l], conv_qkv_w[l], a_log[l], dt_bias[l], delta_norm_g[l], conf_dw_w[l],
              conf_dw_b[l], conf_ln_g[l], conf_ln_b[l], w_out[l], g_pre_mix[l], g_post_mix[l],
              g_pre_ffn[l], g_post_ffn[l], w_up[l], w_down[l])
        xp, s_p, bq_p, bc_p = hybrid_layer(
            xp, jnp.zeros((B, N_HEADS, HEAD_K, HEAD_V), state_delta.dtype),
            jnp.zeros((B, SHORT_CONV - 1, QKV_WIDTH), x_prompt.dtype),
            jnp.zeros((B, CONF_KERNEL - 1, CONF_WIDTH), x_prompt.dtype), *lw)
        xs, s_s, bq_s, bc_s = hybrid_layer(
            xs, state_delta[l], state_qkv_conv[l], state_conf_conv[l], *lw)
        dp.append(s_p)
        qp.append(bq_p)
        cp.append(bc_p)
        ds.append(s_s)
        qs.append(bq_s)
        cs.append(bc_s)
    return (xp, xs, jnp.stack(dp), jnp.stack(qp), jnp.stack(cp),
            jnp.stack(ds), jnp.stack(qs), jnp.stack(cs))
```

```python
import functools

import jax
import jax.numpy as jnp
from jax import lax
from jax.experimental import pallas as pl
from jax.experimental.pallas import tpu as pltpu

D_MODEL = 1024
N_HEADS = 8
HEAD = 128
SHORT_CONV = 4
CONF_KERNEL = 31
D_FF = 4 * D_MODEL
EPS = 1e-6
CHUNK = 64

QKV_TAIL = 8
CONF_TAIL = 32

_C_Q, _C_K, _C_V, _C_Z, _C_GLU_A, _C_GLU_B, _C_GATE_A, _C_GATE_B, _C_BD = (
    0, 1024, 2048, 3072, 4096, 5120, 6144, 7168, 8192)
W_CAT = _C_BD + 128

VMEM_LIMIT_BYTES = 56 * 1024 * 1024

F32 = jnp.float32
BF16 = jnp.bfloat16
HI = lax.Precision.HIGHEST

_NN = (((1,), (0,)), ((), ()))
_NT = (((1,), (1,)), ((), ()))
_TN = (((0,), (0,)), ((), ()))


def _mm(a, b, dims=_NN, exact=False):
    if exact:
        return lax.dot_general(a, b, dims, precision=HI, preferred_element_type=F32)
    return lax.dot_general(a.astype(BF16), b.astype(BF16), dims, preferred_element_type=F32)


def _sigmoid(x):
    return 1.0 / (1.0 + jnp.exp(-x))


def _rms(x, g):
    return x * lax.rsqrt(jnp.mean(x * x, axis=-1, keepdims=True) + EPS) * g


def _proj_kernel(x_ref, qkv0_ref, conf0_ref, w_ref, gpre_ref, cqkv_ref, decay_ref, cdw_ref, cvec_ref,
                 q_ref, k_ref, v_ref, bg_ref, zs_ref, ga_ref, obg_ref, qtail_ref, ctail_ref,
                 extq, extc, *, nb, tl):
    r = nb * tl

    @pl.when(pl.program_id(1) == 0)
    def _():
        extq[:, 0:QKV_TAIL, :] = qkv0_ref[...]
        extc[:, 0:CONF_TAIL, :] = conf0_ref[...]

    x = x_ref[...].reshape(r, D_MODEL)
    hb = _rms(x, gpre_ref[...]).astype(BF16)

    def proj(c0, width=D_MODEL):
        return jnp.dot(hb, w_ref[:, c0:c0 + width], preferred_element_type=F32)

    for c0 in (_C_Q, _C_K, _C_V):
        extq[:, QKV_TAIL:QKV_TAIL + tl, c0:c0 + D_MODEL] = proj(c0).reshape(nb, tl, D_MODEL)
    for c0, out_ref in ((_C_Q, q_ref), (_C_K, k_ref), (_C_V, v_ref)):
        acc = None
        for j in range(SHORT_CONV):
            lo = QKV_TAIL - (SHORT_CONV - 1) + j
            term = extq[:, lo:lo + tl, c0:c0 + D_MODEL] * cqkv_ref[j:j + 1, c0:c0 + D_MODEL]
            acc = term if acc is None else acc + term
        a = acc.reshape(r, D_MODEL)
        a = a * _sigmoid(a)
        if out_ref is v_ref:
            out_ref[...] = a.reshape(nb, tl, D_MODEL)
        else:
            scale = HEAD ** -0.5 if out_ref is q_ref else 1.0
            for h in range(N_HEADS):
                ah = a[:, h * HEAD:(h + 1) * HEAD]
                n = lax.rsqrt(jnp.sum(ah * ah, axis=-1, keepdims=True) + EPS)
                if scale != 1.0:
                    ah = (ah * n) * scale
                else:
                    ah = ah * n
                out_ref[:, :, h * HEAD:(h + 1) * HEAD] = ah.reshape(nb, tl, HEAD)
    qtail = extq[:, tl:tl + QKV_TAIL, :]
    qtail_ref[...] = qtail
    extq[:, 0:QKV_TAIL, :] = qtail

    bd = proj(_C_BD, 128)
    lane = lax.broadcasted_iota(jnp.int32, bd.shape, 1)
    xg = bd + decay_ref[1:2, :]
    softplus = jnp.maximum(xg, 0.0) + jnp.log1p(jnp.exp(-jnp.abs(xg)))
    g = -jnp.exp(decay_ref[0:1, :]) * softplus
    bg_ref[...] = jnp.where(lane < N_HEADS, _sigmoid(bd), g).reshape(nb, tl, 128)

    z = proj(_C_Z)
    zs_ref[...] = (z * _sigmoid(z)).reshape(nb, tl, D_MODEL)

    glu = proj(_C_GLU_A) * _sigmoid(proj(_C_GLU_B))
    extc[:, CONF_TAIL:CONF_TAIL + tl, :] = glu.reshape(nb, tl, D_MODEL)
    acc = None
    for j in range(CONF_KERNEL):
        lo = CONF_TAIL - (CONF_KERNEL - 1) + j
        term = extc[:, lo:lo + tl, :] * cdw_ref[j:j + 1, :]
        acc = term if acc is None else acc + term
    c = acc.reshape(r, D_MODEL) + cvec_ref[0:1, :]
    mu = jnp.mean(c, axis=-1, keepdims=True)
    xc = c - mu
    var = jnp.mean(xc * xc, axis=-1, keepdims=True)
    y = xc * lax.rsqrt(var + EPS) * cvec_ref[1:2, :] + cvec_ref[2:3, :]
    ob = y * _sigmoid(y)
    obg_ref[...] = (_sigmoid(proj(_C_GATE_B)) * ob).reshape(nb, tl, D_MODEL)
    ga_ref[...] = _sigmoid(proj(_C_GATE_A)).reshape(nb, tl, D_MODEL)
    ctail = extc[:, tl:tl + CONF_TAIL, :]
    ctail_ref[...] = ctail
    extc[:, 0:CONF_TAIL, :] = ctail


def _const_spec(shape):
    zeros = (0,) * len(shape)
    return pl.BlockSpec(shape, lambda *_: zeros, pipeline_mode=pl.Buffered(1))


def _proj_call(x, qkv0, conf0, w_cat, gpre, cqkv, decay, cdw, cvec, *, nb, tl):
    b, l, _ = x.shape
    grid = (b // nb, l // tl)
    seq = lambda width: pl.BlockSpec((nb, tl, width), lambda i, t: (i, t, 0))
    per_b = lambda rows, width: pl.BlockSpec((nb, rows, width), lambda i, t: (i, 0, 0))
    big = jax.ShapeDtypeStruct((b, l, D_MODEL), F32)
    return pl.pallas_call(
        functools.partial(_proj_kernel, nb=nb, tl=tl),
        grid=grid,
        in_specs=[seq(D_MODEL), per_b(QKV_TAIL, 3 * D_MODEL), per_b(CONF_TAIL, D_MODEL),
                  _const_spec(w_cat.shape), _const_spec(gpre.shape), _const_spec(cqkv.shape),
                  _const_spec(decay.shape), _const_spec(cdw.shape), _const_spec(cvec.shape)],
        out_specs=[seq(D_MODEL), seq(D_MODEL), seq(D_MODEL), seq(128), seq(D_MODEL), seq(D_MODEL),
                   seq(D_MODEL), per_b(QKV_TAIL, 3 * D_MODEL), per_b(CONF_TAIL, D_MODEL)],
        out_shape=[big, big, big, jax.ShapeDtypeStruct((b, l, 128), F32), big, big, big,
                   jax.ShapeDtypeStruct((b, QKV_TAIL, 3 * D_MODEL), F32),
                   jax.ShapeDtypeStruct((b, CONF_TAIL, D_MODEL), F32)],
        scratch_shapes=[pltpu.VMEM((nb, QKV_TAIL + tl, 3 * D_MODEL), F32),
                        pltpu.VMEM((nb, CONF_TAIL + tl, D_MODEL), F32)],
        compiler_params=pltpu.CompilerParams(
            dimension_semantics=("arbitrary", "arbitrary"), vmem_limit_bytes=VMEM_LIMIT_BYTES),
        name="proj",
    )(x, qkv0, conf0, w_cat, gpre, cqkv, decay, cdw, cvec)


def _delta_kernel(q_ref, k_ref, v_ref, bg_ref, s0_ref, o_ref, sout_ref, s_scr, *, c):
    ci = pl.program_id(1)

    @pl.when(ci == 0)
    def _():
        s_scr[...] = s0_ref[0]

    bg = bg_ref[0]
    row = lax.broadcasted_iota(jnp.int32, (c, c), 0)
    col = lax.broadcasted_iota(jnp.int32, (c, c), 1)
    causal = row >= col
    strict = row > col
    gam = _mm(causal.astype(F32), bg, exact=True)
    sel_r = lax.broadcasted_iota(jnp.int32, (N_HEADS, 128), 0)
    sel_c = lax.broadcasted_iota(jnp.int32, (N_HEADS, 128), 1)
    sel = (sel_c == sel_r + N_HEADS).astype(F32)
    gam_t = _mm(sel, gam, _NT, exact=True)

    for h in range(N_HEADS):
        sl = slice(h * HEAD, (h + 1) * HEAD)
        q = q_ref[0, :, sl]
        k = k_ref[0, :, sl]
        v = v_ref[0, :, sl]
        beta = bg[:, h:h + 1]
        gc = gam[:, N_HEADS + h:N_HEADS + h + 1]
        gr = gam_t[h:h + 1, :]
        glast = gam[c - 1:c, N_HEADS + h:N_HEADS + h + 1]
        decay = jnp.where(causal, jnp.exp(jnp.where(causal, gc - gr, 0.0)), 0.0)
        kk = _mm(k, k, _NT, exact=True)
        qk = _mm(q, k, _NT, exact=True)
        n = jnp.where(strict, -(beta * kk * decay), 0.0)
        eg = jnp.exp(gc)
        sol = jnp.concatenate([beta * v, (beta * eg) * k], axis=-1)
        sol = sol + _mm(n, sol, exact=True)
        p = n
        span = 2
        while span < c:
            p = _mm(p, p, exact=True)
            sol = sol + _mm(p, sol, exact=True)
            span *= 2
        vw = sol[:, :HEAD]
        kd = sol[:, HEAD:]
        s = s_scr[h]
        u = vw - _mm(kd, s, exact=True)
        o = _mm(q * eg, s, exact=True) + _mm(qk * decay, u, exact=True)
        ko = k * jnp.exp(glast - gc)
        s_scr[h] = jnp.exp(glast) * s + _mm(ko, u, _TN, exact=True)
        o_ref[0, :, sl] = o

    @pl.when(ci == pl.num_programs(1) - 1)
    def _():
        sout_ref[0] = s_scr[...]


def _delta_call(q, k, v, bg, s0, *, c):
    b, l, _ = q.shape
    seq = lambda width: pl.BlockSpec((1, c, width), lambda i, t: (i, t, 0))
    st = pl.BlockSpec((1, N_HEADS, HEAD, HEAD), lambda i, t: (i, 0, 0, 0))
    return pl.pallas_call(
        functools.partial(_delta_kernel, c=c),
        grid=(b, l // c),
        in_specs=[seq(D_MODEL), seq(D_MODEL), seq(D_MODEL), seq(128), st],
        out_specs=[seq(D_MODEL), st],
        out_shape=[jax.ShapeDtypeStruct((b, l, D_MODEL), F32),
                   jax.ShapeDtypeStruct((b, N_HEADS, HEAD, HEAD), F32)],
        scratch_shapes=[pltpu.VMEM((N_HEADS, HEAD, HEAD), F32)],
        compiler_params=pltpu.CompilerParams(
            dimension_semantics=("arbitrary", "arbitrary"), vmem_limit_bytes=VMEM_LIMIT_BYTES),
        name="delta",
    )(q, k, v, bg, s0)


FF_SPLIT = 4


def _out_kernel(x_ref, o_ref, zs_ref, ga_ref, obg_ref, wout_ref, wup_ref, wdn_ref, vec_ref, y_ref):
    o = o_ref[...]
    parts = []
    for h in range(N_HEADS):
        oh = o[:, h * HEAD:(h + 1) * HEAD]
        parts.append(oh * lax.rsqrt(jnp.mean(oh * oh, axis=-1, keepdims=True) + EPS))
    oa = jnp.concatenate(parts, axis=-1) * vec_ref[0:1, :] * zs_ref[...]
    mixed = ga_ref[...] * oa + obg_ref[...]
    a = jnp.dot(mixed.astype(BF16), wout_ref[...], preferred_element_type=F32)
    x1 = x_ref[...] + _rms(a, vec_ref[1:2, :])
    hn = _rms(x1, vec_ref[2:3, :]).astype(BF16)
    fw = D_FF // FF_SPLIT
    acc = None
    for i in range(FF_SPLIT):
        hf = jnp.dot(hn, wup_ref[:, i * fw:(i + 1) * fw], preferred_element_type=F32)
        f = jnp.square(jnp.maximum(hf, 0.0)).astype(BF16)
        d = jnp.dot(f, wdn_ref[i * fw:(i + 1) * fw, :], preferred_element_type=F32)
        acc = d if acc is None else acc + d
    y_ref[...] = x1 + _rms(acc, vec_ref[3:4, :])


def _out_call(x, o, zs, ga, obg, wout, wup, wdn, vec, *, tm):
    n = x.shape[0]
    tok = pl.BlockSpec((tm, D_MODEL), lambda i: (i, 0))
    return pl.pallas_call(
        _out_kernel,
        grid=(n // tm,),
        in_specs=[tok, tok, tok, tok, tok, _const_spec(wout.shape), _const_spec(wup.shape),
                  _const_spec(wdn.shape), _const_spec(vec.shape)],
        out_specs=tok,
        out_shape=jax.ShapeDtypeStruct((n, D_MODEL), F32),
        compiler_params=pltpu.CompilerParams(
            dimension_semantics=("arbitrary",), vmem_limit_bytes=VMEM_LIMIT_BYTES),
        name="out",
    )(x, o, zs, ga, obg, wout, wup, wdn, vec)


def _layer(x, s0, qkv0, conf0, weights, *, nb, tl, c, tm):
    w_cat, gpre, cqkv, decay, cdw, cvec, wout, wup, wdn, vec = weights
    b, l, _ = x.shape
    qkv0 = jnp.pad(qkv0, ((0, 0), (QKV_TAIL - (SHORT_CONV - 1), 0), (0, 0)))
    conf0 = jnp.pad(conf0, ((0, 0), (CONF_TAIL - (CONF_KERNEL - 1), 0), (0, 0)))
    q, k, v, bg, zs, ga, obg, qtail, ctail = _proj_call(
        x, qkv0, conf0, w_cat, gpre, cqkv, decay, cdw, cvec, nb=nb, tl=tl)
    o, s_new = _delta_call(q, k, v, bg, s0, c=c)
    flat = lambda t: t.reshape(b * l, D_MODEL)
    y = _out_call(flat(x), flat(o), flat(zs), flat(ga), flat(obg), wout, wup, wdn, vec, tm=tm)
    return (y.reshape(b, l, D_MODEL), s_new,
            qtail[:, QKV_TAIL - (SHORT_CONV - 1):], ctail[:, CONF_TAIL - (CONF_KERNEL - 1):])


def _prep_weights(w_in, conv_qkv_w, a_log, dt_bias, delta_norm_g, conf_dw_w, conf_dw_b, conf_ln_g,
                  conf_ln_b, w_out, g_pre_mix, g_post_mix, g_pre_ffn, g_post_ffn, w_up, w_down):
    qkv_w = 3 * D_MODEL
    o_bd = qkv_w
    o_z = o_bd + 2 * N_HEADS
    w_cat = jnp.concatenate(
        [w_in[:, :qkv_w], w_in[:, o_z:], w_in[:, o_bd:o_z],
         jnp.zeros((D_MODEL, 128 - 2 * N_HEADS), w_in.dtype)], axis=1).astype(BF16)
    pad_heads = lambda t: jnp.pad(t, (N_HEADS, 128 - 2 * N_HEADS))
    decay = jnp.stack([pad_heads(a_log), pad_heads(dt_bias)]).astype(F32)
    cdw = jnp.pad(conf_dw_w, ((0, 32 - CONF_KERNEL), (0, 0)))
    cvec = jnp.stack([conf_dw_b, conf_ln_g, conf_ln_b])
    vec = jnp.stack([jnp.tile(delta_norm_g, N_HEADS), g_post_mix, g_pre_ffn, g_post_ffn])
    return (w_cat, g_pre_mix[None, :], conv_qkv_w, decay, cdw, cvec,
            w_out.astype(BF16), w_up.astype(BF16), w_down.astype(BF16), vec)


def kernel(x_prompt, x_sample, state_delta, state_qkv_conv, state_conf_conv, w_in, conv_qkv_w, a_log,
           dt_bias, delta_norm_g, conf_dw_w, conf_dw_b, conf_ln_g, conf_ln_b, w_out, g_pre_mix,
           g_post_mix, g_pre_ffn, g_post_ffn, w_up, w_down):
    depth = w_in.shape[0]
    bp = x_prompt.shape[0]
    xp, xs = x_prompt, x_sample
    outs = [[] for _ in range(6)]
    for l in range(depth):
        weights = _prep_weights(
            w_in[l], conv_qkv_w[l], a_log[l], dt_bias[l], delta_norm_g[l], conf_dw_w[l], conf_dw_b[l],
            conf_ln_g[l], conf_ln_b[l], w_out[l], g_pre_mix[l], g_post_mix[l], g_pre_ffn[l],
            g_post_ffn[l], w_up[l], w_down[l])
        xp, s_p, bq_p, bc_p = _layer(
            xp, jnp.zeros((bp, N_HEADS, HEAD, HEAD), state_delta.dtype),
            jnp.zeros((bp, SHORT_CONV - 1, 3 * D_MODEL), F32),
            jnp.zeros((bp, CONF_KERNEL - 1, D_MODEL), F32), weights,
            nb=1, tl=256, c=CHUNK, tm=256)
        xs, s_s, bq_s, bc_s = _layer(
            xs, state_delta[l], state_qkv_conv[l], state_conf_conv[l], weights,
            nb=16, tl=x_sample.shape[1], c=x_sample.shape[1], tm=256)
        for lst, val in zip(outs, (s_p, bq_p, bc_p, s_s, bq_s, bc_s)):
            lst.append(val)
    return (xp, xs) + tuple(jnp.stack(o) for o in outs)
```

```python
import functools

import jax
import jax.numpy as jnp
from jax import lax
from jax.experimental import pallas as pl
from jax.experimental.pallas import tpu as pltpu

D_MODEL = 1024
N_HEADS = 8
HEAD = 128
SHORT_CONV = 4
CONF_KERNEL = 31
D_FF = 4 * D_MODEL
EPS = 1e-6
CHUNK = 64

QKV_TAIL = 8
CONF_TAIL = 32

_C_Q, _C_K, _C_V, _C_Z, _C_GLU_A, _C_GLU_B, _C_GATE_A, _C_GATE_B, _C_BD = (
    0, 1024, 2048, 3072, 4096, 5120, 6144, 7168, 8192)
W_CAT = _C_BD + 128

VMEM_LIMIT_BYTES = 56 * 1024 * 1024

F32 = jnp.float32
BF16 = jnp.bfloat16
HI = lax.Precision.HIGHEST

_NN = (((1,), (0,)), ((), ()))
_NT = (((1,), (1,)), ((), ()))
_TN = (((0,), (0,)), ((), ()))


def _mm(a, b, dims=_NN, exact=False):
    if exact:
        return lax.dot_general(a, b, dims, precision=HI, preferred_element_type=F32)
    return lax.dot_general(a.astype(BF16), b.astype(BF16), dims, preferred_element_type=F32)


def _sigmoid(x):
    return 1.0 / (1.0 + jnp.exp(-x))


def _rms(x, g):
    return x * lax.rsqrt(jnp.mean(x * x, axis=-1, keepdims=True) + EPS) * g


def _proj_kernel(x_ref, qkv0_ref, conf0_ref, w_ref, gpre_ref, cqkv_ref, decay_ref, cdw_ref, cvec_ref,
                 q_ref, k_ref, v_ref, bg_ref, zs_ref, ga_ref, obg_ref, qtail_ref, ctail_ref,
                 extq, extc, *, nb, tl):
    r = nb * tl

    @pl.when(pl.program_id(1) == 0)
    def _():
        extq[:, 0:QKV_TAIL, :] = qkv0_ref[...]
        extc[:, 0:CONF_TAIL, :] = conf0_ref[...]

    x = x_ref[...].reshape(r, D_MODEL)
    hb = _rms(x, gpre_ref[...]).astype(BF16)

    def proj(c0, width=D_MODEL):
        return jnp.dot(hb, w_ref[:, c0:c0 + width], preferred_element_type=F32)

    for c0 in (_C_Q, _C_K, _C_V):
        extq[:, QKV_TAIL:QKV_TAIL + tl, c0:c0 + D_MODEL] = proj(c0).reshape(nb, tl, D_MODEL)
    for c0, out_ref in ((_C_Q, q_ref), (_C_K, k_ref), (_C_V, v_ref)):
        acc = None
        for j in range(SHORT_CONV):
            lo = QKV_TAIL - (SHORT_CONV - 1) + j
            term = extq[:, lo:lo + tl, c0:c0 + D_MODEL] * cqkv_ref[j:j + 1, c0:c0 + D_MODEL]
            acc = term if acc is None else acc + term
        a = acc.reshape(r, D_MODEL)
        a = a * _sigmoid(a)
        if out_ref is v_ref:
            out_ref[...] = a.reshape(nb, tl, D_MODEL)
        else:
            scale = HEAD ** -0.5 if out_ref is q_ref else 1.0
            for h in range(N_HEADS):
                ah = a[:, h * HEAD:(h + 1) * HEAD]
                n = lax.rsqrt(jnp.sum(ah * ah, axis=-1, keepdims=True) + EPS)
                if scale != 1.0:
                    ah = (ah * n) * scale
                else:
                    ah = ah * n
                out_ref[:, :, h * HEAD:(h + 1) * HEAD] = ah.reshape(nb, tl, HEAD)
    qtail = extq[:, tl:tl + QKV_TAIL, :]
    qtail_ref[...] = qtail
    extq[:, 0:QKV_TAIL, :] = qtail

    bd = proj(_C_BD, 128)
    lane = lax.broadcasted_iota(jnp.int32, bd.shape, 1)
    xg = bd + decay_ref[1:2, :]
    softplus = jnp.maximum(xg, 0.0) + jnp.log1p(jnp.exp(-jnp.abs(xg)))
    g = -jnp.exp(decay_ref[0:1, :]) * softplus
    bg_ref[...] = jnp.where(lane < N_HEADS, _sigmoid(bd), g).reshape(nb, tl, 128)

    z = proj(_C_Z)
    zs_ref[...] = (z * _sigmoid(z)).reshape(nb, tl, D_MODEL)

    glu = proj(_C_GLU_A) * _sigmoid(proj(_C_GLU_B))
    extc[:, CONF_TAIL:CONF_TAIL + tl, :] = glu.reshape(nb, tl, D_MODEL)
    acc = None
    for j in range(CONF_KERNEL):
        lo = CONF_TAIL - (CONF_KERNEL - 1) + j
        term = extc[:, lo:lo + tl, :] * cdw_ref[j:j + 1, :]
        acc = term if acc is None else acc + term
    c = acc.reshape(r, D_MODEL) + cvec_ref[0:1, :]
    mu = jnp.mean(c, axis=-1, keepdims=True)
    xc = c - mu
    var = jnp.mean(xc * xc, axis=-1, keepdims=True)
    y = xc * lax.rsqrt(var + EPS) * cvec_ref[1:2, :] + cvec_ref[2:3, :]
    ob = y * _sigmoid(y)
    obg_ref[...] = (_sigmoid(proj(_C_GATE_B)) * ob).reshape(nb, tl, D_MODEL)
    ga_ref[...] = _sigmoid(proj(_C_GATE_A)).reshape(nb, tl, D_MODEL)
    ctail = extc[:, tl:tl + CONF_TAIL, :]
    ctail_ref[...] = ctail
    extc[:, 0:CONF_TAIL, :] = ctail


def _const_spec(shape):
    zeros = (0,) * len(shape)
    return pl.BlockSpec(shape, lambda *_: zeros, pipeline_mode=pl.Buffered(1))


def _proj_call(x, qkv0, conf0, w_cat, gpre, cqkv, decay, cdw, cvec, *, nb, tl):
    b, l, _ = x.shape
    grid = (b // nb, l // tl)
    seq = lambda width: pl.BlockSpec((nb, tl, width), lambda i, t: (i, t, 0))
    per_b = lambda rows, width: pl.BlockSpec((nb, rows, width), lambda i, t: (i, 0, 0))
    big = jax.ShapeDtypeStruct((b, l, D_MODEL), F32)
    return pl.pallas_call(
        functools.partial(_proj_kernel, nb=nb, tl=tl),
        grid=grid,
        in_specs=[seq(D_MODEL), per_b(QKV_TAIL, 3 * D_MODEL), per_b(CONF_TAIL, D_MODEL),
                  _const_spec(w_cat.shape), _const_spec(gpre.shape), _const_spec(cqkv.shape),
                  _const_spec(decay.shape), _const_spec(cdw.shape), _const_spec(cvec.shape)],
        out_specs=[seq(D_MODEL), seq(D_MODEL), seq(D_MODEL), seq(128), seq(D_MODEL), seq(D_MODEL),
                   seq(D_MODEL), per_b(QKV_TAIL, 3 * D_MODEL), per_b(CONF_TAIL, D_MODEL)],
        out_shape=[big, big, big, jax.ShapeDtypeStruct((b, l, 128), F32), big, big, big,
                   jax.ShapeDtypeStruct((b, QKV_TAIL, 3 * D_MODEL), F32),
                   jax.ShapeDtypeStruct((b, CONF_TAIL, D_MODEL), F32)],
        scratch_shapes=[pltpu.VMEM((nb, QKV_TAIL + tl, 3 * D_MODEL), F32),
                        pltpu.VMEM((nb, CONF_TAIL + tl, D_MODEL), F32)],
        compiler_params=pltpu.CompilerParams(
            dimension_semantics=("arbitrary", "arbitrary"), vmem_limit_bytes=VMEM_LIMIT_BYTES),
        name="proj",
    )(x, qkv0, conf0, w_cat, gpre, cqkv, decay, cdw, cvec)


def _delta_kernel(q_ref, k_ref, v_ref, bg_ref, s0_ref, o_ref, sout_ref, s_scr, *, nb, c):
    ci = pl.program_id(1)

    @pl.when(ci == 0)
    def _():
        s_scr[...] = s0_ref[...]

    row = lax.broadcasted_iota(jnp.int32, (c, c), 0)
    col = lax.broadcasted_iota(jnp.int32, (c, c), 1)
    causal = row >= col
    strict = row > col
    sel_r = lax.broadcasted_iota(jnp.int32, (N_HEADS, 128), 0)
    sel_c = lax.broadcasted_iota(jnp.int32, (N_HEADS, 128), 1)
    sel = (sel_c == sel_r + N_HEADS).astype(F32)

    chains = [(bi, h) for bi in range(nb) for h in range(N_HEADS)]
    each = lambda f, *lists: [f(*args) for args in zip(*lists)]
    hsl = lambda h: slice(h * HEAD, (h + 1) * HEAD)

    bgs = [bg_ref[bi] for bi in range(nb)]
    gams = [_mm(causal.astype(F32), bg, exact=True) for bg in bgs]
    gam_ts = [_mm(sel, gam, _NT, exact=True) for gam in gams]
    q = [q_ref[bi, :, hsl(h)] for bi, h in chains]
    k = [k_ref[bi, :, hsl(h)] for bi, h in chains]
    v = [v_ref[bi, :, hsl(h)] for bi, h in chains]
    beta = [bgs[bi][:, h:h + 1] for bi, h in chains]
    gc = [gams[bi][:, N_HEADS + h:N_HEADS + h + 1] for bi, h in chains]
    gr = [gam_ts[bi][h:h + 1, :] for bi, h in chains]
    glast = [gams[bi][c - 1:c, N_HEADS + h:N_HEADS + h + 1] for bi, h in chains]
    decay = each(lambda a, b: jnp.where(causal, jnp.exp(jnp.where(causal, a - b, 0.0)), 0.0), gc, gr)
    kb = each(lambda a: a.astype(BF16), k)
    kk = each(lambda a: _mm(a, a, _NT), kb)
    qk = each(lambda a, b: _mm(a, b, _NT), q, kb)
    n = each(lambda b, a, d: jnp.where(strict, -(b * a * d), 0.0), beta, kk, decay)
    eg = each(jnp.exp, gc)
    sol = each(lambda b, vv, e, kk_: jnp.concatenate([b * vv, (b * e) * kk_], axis=-1), beta, v, eg, k)
    sol = each(lambda a, x: x + _mm(a, x), n, sol)
    p = n
    span = 2
    while span < c:
        p = each(lambda a: _mm(a, a), p)
        sol = each(lambda a, x: x + _mm(a, x), p, sol)
        span *= 2
    s = [s_scr[bi, h] for bi, h in chains]
    sb = each(lambda a: a.astype(BF16), s)
    u = each(lambda x, a: x[:, :HEAD] - _mm(x[:, HEAD:], a), sol, sb)
    o = each(lambda qq, e, a, qk_, d, uu: _mm(qq * e, a) + _mm(qk_ * d, uu), q, eg, sb, qk, decay, u)
    ko = each(lambda kk_, gl, g: kk_ * jnp.exp(gl - g), k, glast, gc)
    s_new = each(lambda gl, a, kk_, uu: jnp.exp(gl) * a + _mm(kk_, uu, _TN), glast, s, ko, u)
    for (bi, h), o_h, s_h in zip(chains, o, s_new):
        o_ref[bi, :, hsl(h)] = o_h
        s_scr[bi, h] = s_h

    @pl.when(ci == pl.num_programs(1) - 1)
    def _():
        sout_ref[...] = s_scr[...]


def _delta_call(q, k, v, bg, s0, *, nb, c):
    b, l, _ = q.shape
    seq = lambda width: pl.BlockSpec((nb, c, width), lambda i, t: (i, t, 0))
    st = pl.BlockSpec((nb, N_HEADS, HEAD, HEAD), lambda i, t: (i, 0, 0, 0))
    return pl.pallas_call(
        functools.partial(_delta_kernel, nb=nb, c=c),
        grid=(b // nb, l // c),
        in_specs=[seq(D_MODEL), seq(D_MODEL), seq(D_MODEL), seq(128), st],
        out_specs=[seq(D_MODEL), st],
        out_shape=[jax.ShapeDtypeStruct((b, l, D_MODEL), F32),
                   jax.ShapeDtypeStruct((b, N_HEADS, HEAD, HEAD), F32)],
        scratch_shapes=[pltpu.VMEM((nb, N_HEADS, HEAD, HEAD), F32)],
        compiler_params=pltpu.CompilerParams(
            dimension_semantics=("arbitrary", "arbitrary"), vmem_limit_bytes=VMEM_LIMIT_BYTES),
        name="delta",
    )(q, k, v, bg, s0)


FF_SPLIT = 4


def _out_kernel(x_ref, o_ref, zs_ref, ga_ref, obg_ref, wout_ref, wup_ref, wdn_ref, vec_ref, y_ref):
    o = o_ref[...]
    parts = []
    for h in range(N_HEADS):
        oh = o[:, h * HEAD:(h + 1) * HEAD]
        parts.append(oh * lax.rsqrt(jnp.mean(oh * oh, axis=-1, keepdims=True) + EPS))
    oa = jnp.concatenate(parts, axis=-1) * vec_ref[0:1, :] * zs_ref[...]
    mixed = ga_ref[...] * oa + obg_ref[...]
    a = jnp.dot(mixed.astype(BF16), wout_ref[...], preferred_element_type=F32)
    x1 = x_ref[...] + _rms(a, vec_ref[1:2, :])
    hn = _rms(x1, vec_ref[2:3, :]).astype(BF16)
    fw = D_FF // FF_SPLIT
    acc = None
    for i in range(FF_SPLIT):
        hf = jnp.dot(hn, wup_ref[:, i * fw:(i + 1) * fw], preferred_element_type=F32)
        f = jnp.square(jnp.maximum(hf, 0.0)).astype(BF16)
        d = jnp.dot(f, wdn_ref[i * fw:(i + 1) * fw, :], preferred_element_type=F32)
        acc = d if acc is None else acc + d
    y_ref[...] = x1 + _rms(acc, vec_ref[3:4, :])


def _out_call(x, o, zs, ga, obg, wout, wup, wdn, vec, *, tm):
    n = x.shape[0]
    tok = pl.BlockSpec((tm, D_MODEL), lambda i: (i, 0))
    return pl.pallas_call(
        _out_kernel,
        grid=(n // tm,),
        in_specs=[tok, tok, tok, tok, tok, _const_spec(wout.shape), _const_spec(wup.shape),
                  _const_spec(wdn.shape), _const_spec(vec.shape)],
        out_specs=tok,
        out_shape=jax.ShapeDtypeStruct((n, D_MODEL), F32),
        compiler_params=pltpu.CompilerParams(
            dimension_semantics=("arbitrary",), vmem_limit_bytes=VMEM_LIMIT_BYTES),
        name="out",
    )(x, o, zs, ga, obg, wout, wup, wdn, vec)


def _layer(x, s0, qkv0, conf0, weights, *, nb, tl, nbd, c, tm):
    w_cat, gpre, cqkv, decay, cdw, cvec, wout, wup, wdn, vec = weights
    b, l, _ = x.shape
    qkv0 = jnp.pad(qkv0, ((0, 0), (QKV_TAIL - (SHORT_CONV - 1), 0), (0, 0)))
    conf0 = jnp.pad(conf0, ((0, 0), (CONF_TAIL - (CONF_KERNEL - 1), 0), (0, 0)))
    q, k, v, bg, zs, ga, obg, qtail, ctail = _proj_call(
        x, qkv0, conf0, w_cat, gpre, cqkv, decay, cdw, cvec, nb=nb, tl=tl)
    o, s_new = _delta_call(q, k, v, bg, s0, nb=nbd, c=c)
    flat = lambda t: t.reshape(b * l, D_MODEL)
    y = _out_call(flat(x), flat(o), flat(zs), flat(ga), flat(obg), wout, wup, wdn, vec, tm=tm)
    return (y.reshape(b, l, D_MODEL), s_new,
            qtail[:, QKV_TAIL - (SHORT_CONV - 1):], ctail[:, CONF_TAIL - (CONF_KERNEL - 1):])


def _prep_weights(w_in, conv_qkv_w, a_log, dt_bias, delta_norm_g, conf_dw_w, conf_dw_b, conf_ln_g,
                  conf_ln_b, w_out, g_pre_mix, g_post_mix, g_pre_ffn, g_post_ffn, w_up, w_down):
    qkv_w = 3 * D_MODEL
    o_bd = qkv_w
    o_z = o_bd + 2 * N_HEADS
    w_cat = jnp.concatenate(
        [w_in[:, :qkv_w], w_in[:, o_z:], w_in[:, o_bd:o_z],
         jnp.zeros((D_MODEL, 128 - 2 * N_HEADS), w_in.dtype)], axis=1).astype(BF16)
    pad_heads = lambda t: jnp.pad(t, (N_HEADS, 128 - 2 * N_HEADS))
    decay = jnp.stack([pad_heads(a_log), pad_heads(dt_bias)]).astype(F32)
    cdw = jnp.pad(conf_dw_w, ((0, 32 - CONF_KERNEL), (0, 0)))
    cvec = jnp.stack([conf_dw_b, conf_ln_g, conf_ln_b])
    vec = jnp.stack([jnp.tile(delta_norm_g, N_HEADS), g_post_mix, g_pre_ffn, g_post_ffn])
    return (w_cat, g_pre_mix[None, :], conv_qkv_w, decay, cdw, cvec,
            w_out.astype(BF16), w_up.astype(BF16), w_down.astype(BF16), vec)


def kernel(x_prompt, x_sample, state_delta, state_qkv_conv, state_conf_conv, w_in, conv_qkv_w, a_log,
           dt_bias, delta_norm_g, conf_dw_w, conf_dw_b, conf_ln_g, conf_ln_b, w_out, g_pre_mix,
           g_post_mix, g_pre_ffn, g_post_ffn, w_up, w_down):
    depth = w_in.shape[0]
    bp = x_prompt.shape[0]
    xp, xs = x_prompt, x_sample
    outs = [[] for _ in range(6)]
    for l in range(depth):
        weights = _prep_weights(
            w_in[l], conv_qkv_w[l], a_log[l], dt_bias[l], delta_norm_g[l], conf_dw_w[l], conf_dw_b[l],
            conf_ln_g[l], conf_ln_b[l], w_out[l], g_pre_mix[l], g_post_mix[l], g_pre_ffn[l],
            g_post_ffn[l], w_up[l], w_down[l])
        xp, s_p, bq_p, bc_p = _layer(
            xp, jnp.zeros((bp, N_HEADS, HEAD, HEAD), state_delta.dtype),
            jnp.zeros((bp, SHORT_CONV - 1, 3 * D_MODEL), F32),
            jnp.zeros((bp, CONF_KERNEL - 1, D_MODEL), F32), weights,
            nb=1, tl=256, nbd=1, c=CHUNK, tm=256)
        xs, s_s, bq_s, bc_s = _layer(
            xs, state_delta[l], state_qkv_conv[l], state_conf_conv[l], weights,
            nb=16, tl=x_sample.shape[1], nbd=4, c=x_sample.shape[1], tm=256)
        for lst, val in zip(outs, (s_p, bq_p, bc_p, s_s, bq_s, bc_s)):
            lst.append(val)
    return (xp, xs) + tuple(jnp.stack(o) for o in outs)
```

```python
import functools

import jax
import jax.numpy as jnp
from jax import lax
from jax.experimental import pallas as pl
from jax.experimental.pallas import tpu as pltpu

D_MODEL = 1024
N_HEADS = 8
HEAD = 128
SHORT_CONV = 4
CONF_KERNEL = 31
D_FF = 4 * D_MODEL
EPS = 1e-6
CHUNK = 64

QKV_TAIL = 8
CONF_TAIL = 32

_C_Q, _C_K, _C_V = 0, D_MODEL, 2 * D_MODEL
_C_Z, _C_GLU_A, _C_GLU_B, _C_GATE_A, _C_GATE_B = (i * D_MODEL for i in range(5))

VMEM_LIMIT_BYTES = 56 * 1024 * 1024

F32 = jnp.float32
BF16 = jnp.bfloat16
HI = lax.Precision.HIGHEST

_NN = (((1,), (0,)), ((), ()))
_NT = (((1,), (1,)), ((), ()))
_TN = (((0,), (0,)), ((), ()))


def _mm(a, b, dims=_NN, exact=False):
    if exact:
        return lax.dot_general(a, b, dims, precision=HI, preferred_element_type=F32)
    return lax.dot_general(a.astype(BF16), b.astype(BF16), dims, preferred_element_type=F32)


def _sigmoid(x):
    return 1.0 / (1.0 + jnp.exp(-x))


def _rms(x, g):
    return x * lax.rsqrt(jnp.mean(x * x, axis=-1, keepdims=True) + EPS) * g


COL_BLOCK = 256


def _tap_groups(tail, taps):
    base = tail - (taps - 1)
    groups = [[o for o in range(base, tail + 1) if o % 8 == res] for res in range(8)]
    return base, [g for g in groups if g]


def _tap_group_sum(ext_ref, w_ref, cols, offs, base, tl):
    res = offs[0] % 8
    rows = tl + 8 if res else tl
    part = None
    for o in offs:
        term = ext_ref[:, o - res:o - res + rows, cols] * w_ref[o - base:o - base + 1, cols]
        part = term if part is None else part + term
    if res:
        nb, _, width = part.shape
        sub = lax.broadcasted_iota(jnp.int32, (nb, tl, width), 1) % 8
        merged = jnp.where(sub >= res, part[:, :tl, :], part[:, 8:, :])
        merged = merged.reshape(nb, tl // 8, 8, width)
        part = pltpu.roll(merged, 8 - res, axis=2).reshape(nb, tl, width)
    return part


def _emit_interleaved(mxu_jobs, vpu_jobs, lead):
    done = 0
    n_m, n_v = len(mxu_jobs), len(vpu_jobs)
    for i, (need, job) in enumerate(vpu_jobs):
        target = min(n_m, max(need, lead + (i * n_m) // n_v))
        while done < target:
            mxu_jobs[done]()
            done += 1
        job()
    while done < n_m:
        mxu_jobs[done]()
        done += 1


def _proj_kernel(x_ref, qkv0_ref, conf0_ref, wqkv_ref, wrest_ref, wbd_ref, gpre_ref, cqkv_ref,
                 decay_ref, cdw_ref, cvec_ref,
                 q_ref, k_ref, v_ref, bg_ref, zs_ref, ga_ref, obg_ref, qtail_ref, ctail_ref,
                 extq, extc, *, nb, tl):
    r = nb * tl
    nq = SHORT_CONV - 1
    nc = CONF_KERNEL - 1
    n_cb = D_MODEL // COL_BLOCK
    blocks = range(n_cb)
    cb_cols = lambda cb: slice(cb * COL_BLOCK, (cb + 1) * COL_BLOCK)

    @pl.when(pl.program_id(1) == 0)
    def _():
        extq[:, 0:QKV_TAIL, :] = jnp.zeros((nb, QKV_TAIL, 3 * D_MODEL), F32)
        extq[:, QKV_TAIL - nq:QKV_TAIL, :] = qkv0_ref[...]
        extc[:, 0:8, :] = jnp.zeros((nb, 8, D_MODEL), F32)
        extc[:, CONF_TAIL - nc:CONF_TAIL, :] = conf0_ref[...]

    x = x_ref[...].reshape(r, D_MODEL)
    hb = _rms(x, gpre_ref[...]).astype(BF16)
    raw = {}
    proj_jobs = {}

    def add_proj(name, w_ref, c0, width=COL_BLOCK, n_blocks=n_cb, into_extq=False):
        for cb in range(n_blocks):
            def job(cb=cb):
                lo = c0 + cb * width
                val = jnp.dot(hb, w_ref[:, lo:lo + width], preferred_element_type=F32)
                if into_extq:
                    extq[:, QKV_TAIL:QKV_TAIL + tl, lo:lo + width] = val.reshape(nb, tl, width)
                else:
                    raw[name, cb] = val
            proj_jobs[name, cb] = job

    for name, c0 in (("q", _C_Q), ("k", _C_K), ("v", _C_V)):
        add_proj(name, wqkv_ref, c0, into_extq=True)
    for name, c0 in (("glu_a", _C_GLU_A), ("glu_b", _C_GLU_B), ("z", _C_Z), ("gate_b", _C_GATE_B),
                     ("gate_a", _C_GATE_A)):
        add_proj(name, wrest_ref, c0)
    add_proj("bd", wbd_ref, 0, width=128, n_blocks=1)
    mxu_order = ([(n, cb) for cb in blocks for n in ("glu_a", "glu_b")]
                 + [(n, cb) for n in ("q", "k", "v", "z", "gate_a", "gate_b") for cb in blocks]
                 + [("bd", 0)])
    mxu_jobs = [proj_jobs[key] for key in mxu_order]
    mxu_index = {key: i + 1 for i, key in enumerate(mxu_order)}

    def vpu(needs, job):
        return (max(mxu_index[n] for n in needs) if needs else 0, job)

    qbase, qgroups = _tap_groups(QKV_TAIL, SHORT_CONV)
    qkv_jobs = []
    for name, c0, out_ref in (("q", _C_Q, q_ref), ("k", _C_K, k_ref), ("v", _C_V, v_ref)):
        for cb in blocks:
            def job(c0=c0, out_ref=out_ref, cb=cb):
                cols = slice(c0 + cb * COL_BLOCK, c0 + (cb + 1) * COL_BLOCK)
                a = None
                for offs in qgroups:
                    part = _tap_group_sum(extq, cqkv_ref, cols, offs, qbase, tl)
                    a = part if a is None else a + part
                a = a.reshape(r, COL_BLOCK)
                a = a * _sigmoid(a)
                if out_ref is not v_ref:
                    scale = HEAD ** -0.5 if out_ref is q_ref else 1.0
                    heads = []
                    for h in range(COL_BLOCK // HEAD):
                        ah = a[:, h * HEAD:(h + 1) * HEAD]
                        ah = ah * lax.rsqrt(jnp.sum(ah * ah, axis=-1, keepdims=True) + EPS)
                        heads.append(ah * scale if scale != 1.0 else ah)
                    a = jnp.concatenate(heads, axis=-1)
                out_ref[:, :, cb_cols(cb)] = a.reshape(nb, tl, COL_BLOCK)
            qkv_jobs.append(vpu([(name, cb)], job))

    glu_jobs = []
    for cb in blocks:
        def job(cb=cb):
            glu = raw["glu_a", cb] * _sigmoid(raw["glu_b", cb])
            extc[:, CONF_TAIL:CONF_TAIL + tl, cb_cols(cb)] = glu.reshape(nb, tl, COL_BLOCK)
        glu_jobs.append(vpu([("glu_a", cb), ("glu_b", cb)], job))
    cbase, cgroups = _tap_groups(CONF_TAIL, CONF_KERNEL)
    conv = [None] * n_cb
    conv_jobs = []
    for cb in blocks:
        for offs in cgroups:
            def job(cb=cb, offs=offs):
                part = _tap_group_sum(extc, cdw_ref, cb_cols(cb), offs, cbase, tl)
                conv[cb] = part if conv[cb] is None else conv[cb] + part
            conv_jobs.append(vpu([], job))

    def gate_job(name, out_ref, silu, cb):
        def job():
            t = raw[name, cb]
            t = t * _sigmoid(t) if silu else _sigmoid(t)
            out_ref[:, :, cb_cols(cb)] = t.reshape(nb, tl, COL_BLOCK)
        return vpu([(name, cb)], job)

    def beta_decay():
        bd = raw["bd", 0]
        lane = lax.broadcasted_iota(jnp.int32, bd.shape, 1)
        xg = bd + decay_ref[1:2, :]
        softplus = jnp.maximum(xg, 0.0) + jnp.log1p(jnp.exp(-jnp.abs(xg)))
        g = -jnp.exp(decay_ref[0:1, :]) * softplus
        bg_ref[...] = jnp.where(lane < N_HEADS, _sigmoid(bd), g).reshape(nb, tl, 128)

    side_jobs = [gate_job(name, out_ref, silu, cb)
                 for name, out_ref, silu in (("z", zs_ref, True), ("gate_a", ga_ref, False),
                                             ("gate_b", obg_ref, False))
                 for cb in blocks] + [vpu([("bd", 0)], beta_decay)]

    stats = {}

    def ln_stats():
        cs = [conv[cb].reshape(r, COL_BLOCK) + cvec_ref[0:1, cb_cols(cb)] for cb in blocks]
        mu = sum(jnp.sum(c, axis=-1, keepdims=True) for c in cs) * (1.0 / D_MODEL)
        xc = [c - mu for c in cs]
        var = sum(jnp.sum(v * v, axis=-1, keepdims=True) for v in xc) * (1.0 / D_MODEL)
        stats["xc"] = xc
        stats["rstd"] = lax.rsqrt(var + EPS)
    final_jobs = [vpu([], ln_stats)]
    for cb in blocks:
        def job(cb=cb):
            cols = cb_cols(cb)
            y = stats["xc"][cb] * stats["rstd"] * cvec_ref[1:2, cols] + cvec_ref[2:3, cols]
            ob = (y * _sigmoid(y)).reshape(nb, tl, COL_BLOCK)
            obg_ref[:, :, cols] = obg_ref[:, :, cols] * ob
        final_jobs.append(vpu([], job))

    vpu_jobs = [glu_jobs[0]]
    per_block = len(conv_jobs) // n_cb
    n_qkv = len(qkv_jobs)
    for i, cj in enumerate(conv_jobs):
        if i % per_block == 0 and i // per_block + 1 < n_cb:
            vpu_jobs.append(glu_jobs[i // per_block + 1])
        vpu_jobs.append(cj)
        while qkv_jobs and (n_qkv - len(qkv_jobs)) * len(conv_jobs) < n_qkv * (i + 1):
            vpu_jobs.append(qkv_jobs.pop(0))
    vpu_jobs += side_jobs + final_jobs
    _emit_interleaved(mxu_jobs, vpu_jobs, lead=2)

    qtail_ref[...] = extq[:, QKV_TAIL + tl - nq:QKV_TAIL + tl, :]
    extq[:, 0:QKV_TAIL, :] = extq[:, tl:tl + QKV_TAIL, :]
    ctail_ref[...] = extc[:, CONF_TAIL + tl - nc:CONF_TAIL + tl, :]
    extc[:, 0:CONF_TAIL, :] = extc[:, tl:tl + CONF_TAIL, :]


def _const_spec(shape):
    zeros = (0,) * len(shape)
    return pl.BlockSpec(shape, lambda *_: zeros, pipeline_mode=pl.Buffered(1))


def _proj_call(x, qkv0, conf0, wqkv, wrest, wbd, gpre, cqkv, decay, cdw, cvec, *, nb, tl):
    b, l, _ = x.shape
    grid = (b // nb, l // tl)
    seq = lambda width: pl.BlockSpec((nb, tl, width), lambda i, t: (i, t, 0))
    per_b = lambda rows, width: pl.BlockSpec((nb, rows, width), lambda i, t: (i, 0, 0))
    big = jax.ShapeDtypeStruct((b, l, D_MODEL), F32)
    return pl.pallas_call(
        functools.partial(_proj_kernel, nb=nb, tl=tl),
        grid=grid,
        in_specs=[seq(D_MODEL), per_b(SHORT_CONV - 1, 3 * D_MODEL), per_b(CONF_KERNEL - 1, D_MODEL),
                  _const_spec(wqkv.shape), _const_spec(wrest.shape), _const_spec(wbd.shape),
                  _const_spec(gpre.shape), _const_spec(cqkv.shape),
                  _const_spec(decay.shape), _const_spec(cdw.shape), _const_spec(cvec.shape)],
        out_specs=[seq(D_MODEL), seq(D_MODEL), seq(D_MODEL), seq(128), seq(D_MODEL), seq(D_MODEL),
                   seq(D_MODEL), per_b(SHORT_CONV - 1, 3 * D_MODEL), per_b(CONF_KERNEL - 1, D_MODEL)],
        out_shape=[big, big, big, jax.ShapeDtypeStruct((b, l, 128), F32), big, big, big,
                   jax.ShapeDtypeStruct((b, SHORT_CONV - 1, 3 * D_MODEL), F32),
                   jax.ShapeDtypeStruct((b, CONF_KERNEL - 1, D_MODEL), F32)],
        scratch_shapes=[pltpu.VMEM((nb, QKV_TAIL + tl, 3 * D_MODEL), F32),
                        pltpu.VMEM((nb, CONF_TAIL + tl, D_MODEL), F32)],
        compiler_params=pltpu.CompilerParams(
            dimension_semantics=("arbitrary", "arbitrary"), vmem_limit_bytes=VMEM_LIMIT_BYTES),
        name="proj",
    )(x, qkv0, conf0, wqkv, wrest, wbd, gpre, cqkv, decay, cdw, cvec)


def _delta_kernel(q_ref, k_ref, v_ref, bg_ref, s0_ref, o_ref, sout_ref, s_scr, *, nb, c):
    ci = pl.program_id(1)
    n_pairs = N_HEADS // 2
    pw = 2 * HEAD

    @pl.when(ci == 0)
    def _():
        for bi in range(nb):
            for p in range(n_pairs):
                s_scr[bi, p] = jnp.concatenate([s0_ref[bi, 2 * p], s0_ref[bi, 2 * p + 1]], axis=1)

    iota = lambda shape, d: lax.broadcasted_iota(jnp.int32, shape, d)
    row = iota((c, 2 * c), 0)
    lane = iota((c, 2 * c), 1)
    col = lane % c
    first = lane < c
    causal = row >= col
    strict = row > col
    ident = (row == col).astype(F32)
    first_w = iota((c, pw), 1) < HEAD
    tri = (iota((c, c), 0) >= iota((c, c), 1)).astype(F32)
    sel = (iota((n_pairs, 128), 1) // 2 == iota((n_pairs, 128), 0) + N_HEADS // 2).astype(F32)
    odd_lane = iota((c, 128), 1) % 2 == 1

    def bdiag(x, split):
        return jnp.concatenate([jnp.where(split, x, 0.0), jnp.where(split, 0.0, x)], axis=0)

    chains = [(bi, p) for bi in range(nb) for p in range(n_pairs)]
    each = lambda f, *lists: [f(*args) for args in zip(*lists)]
    psl = lambda p: slice(p * pw, (p + 1) * pw)
    pick = lambda cols, p: jnp.where(first, cols[:, 2 * p:2 * p + 1], cols[:, 2 * p + 1:2 * p + 2])
    pick_w = lambda cols, p: jnp.where(first_w[:cols.shape[0]], cols[:, 2 * p:2 * p + 1],
                                       cols[:, 2 * p + 1:2 * p + 2])

    bgs = [bg_ref[bi] for bi in range(nb)]
    gams = [_mm(tri, bg, exact=True) for bg in bgs]
    gam_ts = [_mm(sel, jnp.concatenate([jnp.where(odd_lane, 0.0, g), jnp.where(odd_lane, g, 0.0)], axis=0),
                  _NT, exact=True) for g in gams]
    gcols = [g[:, N_HEADS:2 * N_HEADS] for g in gams]
    q = [q_ref[bi, :, psl(p)] for bi, p in chains]
    k = [k_ref[bi, :, psl(p)] for bi, p in chains]
    v = [v_ref[bi, :, psl(p)] for bi, p in chains]
    beta = [pick(bgs[bi], p) for bi, p in chains]
    beta_w = [pick_w(bgs[bi], p) for bi, p in chains]
    gc = [pick(gcols[bi], p) for bi, p in chains]
    gc_w = [pick_w(gcols[bi], p) for bi, p in chains]
    gl_w = [pick_w(gcols[bi][c - 1:c, :], p) for bi, p in chains]
    gr = [gam_ts[bi][p:p + 1, :] for bi, p in chains]
    decay = each(lambda a, b: jnp.where(causal, jnp.exp(jnp.where(causal, a - b, 0.0)), 0.0), gc, gr)
    kb = each(lambda a: a.astype(BF16), k)
    kkqk = each(lambda a, b: _mm(jnp.concatenate([a, b.astype(BF16)], axis=0), bdiag(a, first_w), _NT),
                kb, q)
    n = each(lambda b, a, d: jnp.where(strict, -(b * a[:c] * d), 0.0), beta, kkqk, decay)
    eg_w = each(jnp.exp, gc_w)
    t = each(lambda a: ident + a, n)
    pw_ = each(lambda a: _mm(a, bdiag(a, first)), n)
    span = 4
    while span <= c:
        pbd = each(lambda a: bdiag(a, first), pw_)
        if span < c:
            both = each(lambda a, b, m: _mm(jnp.concatenate([a, b], axis=0), m), t, pw_, pbd)
            t = each(lambda a, r: a + r[:c], t, both)
            pw_ = each(lambda r: r[c:], both)
        else:
            t = each(lambda a, m: a + _mm(a, m), t, pbd)
        span *= 2
    rhs = each(lambda b, vv, e, kk_: jnp.concatenate(
        [jnp.concatenate([b[:, :HEAD] * vv[:, :HEAD], (b * e * kk_)[:, :HEAD]], axis=1),
         jnp.concatenate([b[:, HEAD:] * vv[:, HEAD:], (b * e * kk_)[:, HEAD:]], axis=1)], axis=0),
        beta_w, v, eg_w, k)
    sol = each(lambda a, r: _mm(bdiag(a, first), r), t, rhs)
    s = [s_scr[bi, p] for bi, p in chains]
    first_s = iota((HEAD, pw), 1) < HEAD
    sbd = each(lambda a: bdiag(a, first_s).astype(BF16), s)
    kd = each(lambda x: jnp.concatenate([x[:c, HEAD:], x[c:, HEAD:]], axis=1), sol)
    vw = each(lambda x: jnp.concatenate([x[:c, :HEAD], x[c:, :HEAD]], axis=1), sol)
    ks_qs = each(lambda a, qq, e, m: _mm(jnp.concatenate([a, qq * e], axis=0), m), kd, q, eg_w, sbd)
    u = each(lambda a, r: a - r[:c], vw, ks_qs)
    ubd = each(lambda a: bdiag(a, first_w).astype(BF16), u)
    o = each(lambda r, a, d, m: r[c:] + _mm(a[c:] * d, m), ks_qs, kkqk, decay, ubd)
    ko = each(lambda kk_, gl, g: kk_ * jnp.exp(gl - g), k, gl_w, gc_w)
    s_new = each(lambda gl, a, kk_, m: jnp.exp(gl) * a + _mm(
        jnp.concatenate([kk_[:, :HEAD], kk_[:, HEAD:]], axis=0), m, _TN), gl_w, s, ko, ubd)
    for (bi, p), o_p, s_p in zip(chains, o, s_new):
        o_ref[bi, :, psl(p)] = o_p
        s_scr[bi, p] = s_p

    @pl.when(ci == pl.num_programs(1) - 1)
    def _():
        for bi in range(nb):
            for p in range(n_pairs):
                sout_ref[bi, 2 * p] = s_scr[bi, p, :, 0:HEAD]
                sout_ref[bi, 2 * p + 1] = s_scr[bi, p, :, HEAD:pw]


def _delta_call(q, k, v, bg, s0, *, nb, c):
    b, l, _ = q.shape
    seq = lambda width: pl.BlockSpec((nb, c, width), lambda i, t: (i, t, 0))
    st = pl.BlockSpec((nb, N_HEADS, HEAD, HEAD), lambda i, t: (i, 0, 0, 0))
    return pl.pallas_call(
        functools.partial(_delta_kernel, nb=nb, c=c),
        grid=(b // nb, l // c),
        in_specs=[seq(D_MODEL), seq(D_MODEL), seq(D_MODEL), seq(128), st],
        out_specs=[seq(D_MODEL), st],
        out_shape=[jax.ShapeDtypeStruct((b, l, D_MODEL), F32),
                   jax.ShapeDtypeStruct((b, N_HEADS, HEAD, HEAD), F32)],
        scratch_shapes=[pltpu.VMEM((nb, N_HEADS // 2, HEAD, 2 * HEAD), F32)],
        compiler_params=pltpu.CompilerParams(
            dimension_semantics=("arbitrary", "arbitrary"), vmem_limit_bytes=VMEM_LIMIT_BYTES),
        name="delta",
    )(q, k, v, bg, s0)


FF_SPLIT = 4


def _out_kernel(x_ref, o_ref, zs_ref, ga_ref, obg_ref, wout_ref, wup_ref, wdn_ref, vec_ref, y_ref):
    o = o_ref[...]
    parts = []
    for h in range(N_HEADS):
        oh = o[:, h * HEAD:(h + 1) * HEAD]
        parts.append(oh * lax.rsqrt(jnp.mean(oh * oh, axis=-1, keepdims=True) + EPS))
    oa = jnp.concatenate(parts, axis=-1) * vec_ref[0:1, :] * zs_ref[...]
    mixed = ga_ref[...] * oa + obg_ref[...]
    a = jnp.dot(mixed.astype(BF16), wout_ref[...], preferred_element_type=F32)
    x1 = x_ref[...] + _rms(a, vec_ref[1:2, :])
    hn = _rms(x1, vec_ref[2:3, :]).astype(BF16)
    fw = D_FF // FF_SPLIT
    acc = None
    for i in range(FF_SPLIT):
        hf = jnp.dot(hn, wup_ref[:, i * fw:(i + 1) * fw], preferred_element_type=F32)
        f = jnp.square(jnp.maximum(hf, 0.0)).astype(BF16)
        d = jnp.dot(f, wdn_ref[i * fw:(i + 1) * fw, :], preferred_element_type=F32)
        acc = d if acc is None else acc + d
    y_ref[...] = x1 + _rms(acc, vec_ref[3:4, :])


def _out_call(x, o, zs, ga, obg, wout, wup, wdn, vec, *, tm):
    n = x.shape[0]
    tok = pl.BlockSpec((tm, D_MODEL), lambda i: (i, 0))
    return pl.pallas_call(
        _out_kernel,
        grid=(n // tm,),
        in_specs=[tok, tok, tok, tok, tok, _const_spec(wout.shape), _const_spec(wup.shape),
                  _const_spec(wdn.shape), _const_spec(vec.shape)],
        out_specs=tok,
        out_shape=jax.ShapeDtypeStruct((n, D_MODEL), F32),
        compiler_params=pltpu.CompilerParams(
            dimension_semantics=("arbitrary",), vmem_limit_bytes=VMEM_LIMIT_BYTES),
        name="out",
    )(x, o, zs, ga, obg, wout, wup, wdn, vec)


def _layer(x, s0, qkv0, conf0, weights, *, nb, tl, nbd, c, tm):
    wqkv, wrest, wbd, gpre, cqkv, decay, cdw, cvec, wout, wup, wdn, vec = weights
    b, l, _ = x.shape
    q, k, v, bg, zs, ga, obg, qtail, ctail = _proj_call(
        x, qkv0, conf0, wqkv, wrest, wbd, gpre, cqkv, decay, cdw, cvec, nb=nb, tl=tl)
    o, s_new = _delta_call(q, k, v, bg, s0, nb=nbd, c=c)
    flat = lambda t: t.reshape(b * l, D_MODEL)
    y = _out_call(flat(x), flat(o), flat(zs), flat(ga), flat(obg), wout, wup, wdn, vec, tm=tm)
    return y.reshape(b, l, D_MODEL), s_new, qtail, ctail


def _prep_weights(w_in, conv_qkv_w, a_log, dt_bias, delta_norm_g, conf_dw_w, conf_dw_b, conf_ln_g,
                  conf_ln_b, w_out, g_pre_mix, g_post_mix, g_pre_ffn, g_post_ffn, w_up, w_down):
    qkv_w = 3 * D_MODEL
    o_bd = qkv_w
    o_z = o_bd + 2 * N_HEADS
    wqkv = w_in[:, :qkv_w].astype(BF16)
    wrest = w_in[:, o_z:].astype(BF16)
    wbd = jnp.pad(w_in[:, o_bd:o_z], ((0, 0), (0, 128 - 2 * N_HEADS))).astype(BF16)
    pad_heads = lambda t: jnp.pad(t, (N_HEADS, 128 - 2 * N_HEADS))
    decay = jnp.stack([pad_heads(a_log), pad_heads(dt_bias)]).astype(F32)
    cvec = jnp.stack([conf_dw_b, conf_ln_g, conf_ln_b])
    vec = jnp.stack([jnp.tile(delta_norm_g, N_HEADS), g_post_mix, g_pre_ffn, g_post_ffn])
    return (wqkv, wrest, wbd, g_pre_mix[None, :], conv_qkv_w, decay, conf_dw_w, cvec,
            w_out.astype(BF16), w_up.astype(BF16), w_down.astype(BF16), vec)


def kernel(x_prompt, x_sample, state_delta, state_qkv_conv, state_conf_conv, w_in, conv_qkv_w, a_log,
           dt_bias, delta_norm_g, conf_dw_w, conf_dw_b, conf_ln_g, conf_ln_b, w_out, g_pre_mix,
           g_post_mix, g_pre_ffn, g_post_ffn, w_up, w_down):
    depth = w_in.shape[0]
    bp = x_prompt.shape[0]
    xp, xs = x_prompt, x_sample
    outs = [[] for _ in range(6)]
    for l in range(depth):
        weights = _prep_weights(
            w_in[l], conv_qkv_w[l], a_log[l], dt_bias[l], delta_norm_g[l], conf_dw_w[l], conf_dw_b[l],
            conf_ln_g[l], conf_ln_b[l], w_out[l], g_pre_mix[l], g_post_mix[l], g_pre_ffn[l],
            g_post_ffn[l], w_up[l], w_down[l])
        xp, s_p, bq_p, bc_p = _layer(
            xp, jnp.zeros((bp, N_HEADS, HEAD, HEAD), state_delta.dtype),
            jnp.zeros((bp, SHORT_CONV - 1, 3 * D_MODEL), F32),
            jnp.zeros((bp, CONF_KERNEL - 1, D_MODEL), F32), weights,
            nb=1, tl=256, nbd=8, c=CHUNK, tm=512)
        xs, s_s, bq_s, bc_s = _layer(
            xs, state_delta[l], state_qkv_conv[l], state_conf_conv[l], weights,
            nb=16, tl=x_sample.shape[1], nbd=8, c=x_sample.shape[1], tm=256)
        for lst, val in zip(outs, (s_p, bq_p, bc_p, s_s, bq_s, bc_s)):
            lst.append(val)
    return (xp, xs) + tuple(jnp.stack(o) for o in outs)
```

```python
import functools

import jax
import jax.numpy as jnp
from jax import lax
from jax.experimental import pallas as pl
from jax.experimental.pallas import tpu as pltpu

D_MODEL = 1024
N_HEADS = 8
HEAD = 128
SHORT_CONV = 4
CONF_KERNEL = 31
D_FF = 4 * D_MODEL
EPS = 1e-6
CHUNK = 64

QKV_TAIL = 8
CONF_TAIL = 32

_C_Q, _C_K, _C_V = 0, D_MODEL, 2 * D_MODEL
_C_Z, _C_GLU_A, _C_GLU_B, _C_GATE_A, _C_GATE_B = (i * D_MODEL for i in range(5))

VMEM_LIMIT_BYTES = 56 * 1024 * 1024

F32 = jnp.float32
BF16 = jnp.bfloat16
HI = lax.Precision.HIGHEST

_NN = (((1,), (0,)), ((), ()))
_NT = (((1,), (1,)), ((), ()))
_TN = (((0,), (0,)), ((), ()))


def _mm(a, b, dims=_NN, exact=False):
    if exact:
        return lax.dot_general(a, b, dims, precision=HI, preferred_element_type=F32)
    return lax.dot_general(a.astype(BF16), b.astype(BF16), dims, preferred_element_type=F32)


def _sigmoid(x):
    return 1.0 / (1.0 + jnp.exp(-x))


def _rms(x, g):
    return x * lax.rsqrt(jnp.mean(x * x, axis=-1, keepdims=True) + EPS) * g


COL_BLOCK = 256


def _tap_groups(tail, taps):
    base = tail - (taps - 1)
    groups = [[o for o in range(base, tail + 1) if o % 8 == res] for res in range(8)]
    return base, [g for g in groups if g]


def _tap_group_sum(ext_ref, w_ref, cols, offs, base, tl):
    res = offs[0] % 8
    rows = tl + 8 if res else tl
    part = None
    for o in offs:
        term = ext_ref[:, o - res:o - res + rows, cols] * w_ref[o - base:o - base + 1, cols]
        part = term if part is None else part + term
    if res:
        nb, _, width = part.shape
        sub = lax.broadcasted_iota(jnp.int32, (nb, tl, width), 1) % 8
        merged = jnp.where(sub >= res, part[:, :tl, :], part[:, 8:, :])
        merged = merged.reshape(nb, tl // 8, 8, width)
        part = pltpu.roll(merged, 8 - res, axis=2).reshape(nb, tl, width)
    return part


def _emit_interleaved(mxu_jobs, vpu_jobs, lead):
    done = 0
    n_m, n_v = len(mxu_jobs), len(vpu_jobs)
    for i, (need, job) in enumerate(vpu_jobs):
        target = min(n_m, max(need, lead + (i * n_m) // n_v))
        while done < target:
            mxu_jobs[done]()
            done += 1
        job()
    while done < n_m:
        mxu_jobs[done]()
        done += 1


def _proj_kernel(x_ref, qkv0_ref, conf0_ref, wqkv_ref, wrest_ref, wbd_ref, gpre_ref, cqkv_ref,
                 decay_ref, cdw_ref, cvec_ref,
                 q_ref, k_ref, v_ref, bg_ref, zs_ref, ga_ref, obg_ref, qtail_ref, ctail_ref,
                 extq, extc, *, nb, tl):
    r = nb * tl
    nq = SHORT_CONV - 1
    nc = CONF_KERNEL - 1
    n_cb = D_MODEL // COL_BLOCK
    blocks = range(n_cb)
    cb_cols = lambda cb: slice(cb * COL_BLOCK, (cb + 1) * COL_BLOCK)

    @pl.when(pl.program_id(1) == 0)
    def _():
        extq[:, 0:QKV_TAIL, :] = jnp.zeros((nb, QKV_TAIL, 3 * D_MODEL), F32)
        extq[:, QKV_TAIL - nq:QKV_TAIL, :] = qkv0_ref[...]
        extc[:, 0:8, :] = jnp.zeros((nb, 8, D_MODEL), F32)
        extc[:, CONF_TAIL - nc:CONF_TAIL, :] = conf0_ref[...]

    x = x_ref[...].reshape(r, D_MODEL)
    hb = _rms(x, gpre_ref[...]).astype(BF16)
    raw = {}
    proj_jobs = {}

    def add_proj(name, w_ref, c0, width=COL_BLOCK, n_blocks=n_cb, into_extq=False):
        for cb in range(n_blocks):
            def job(cb=cb):
                lo = c0 + cb * width
                val = jnp.dot(hb, w_ref[:, lo:lo + width], preferred_element_type=F32)
                if into_extq:
                    extq[:, QKV_TAIL:QKV_TAIL + tl, lo:lo + width] = val.reshape(nb, tl, width)
                else:
                    raw[name, cb] = val
            proj_jobs[name, cb] = job

    for name, c0 in (("q", _C_Q), ("k", _C_K), ("v", _C_V)):
        add_proj(name, wqkv_ref, c0, into_extq=True)
    for name, c0 in (("glu_a", _C_GLU_A), ("glu_b", _C_GLU_B), ("z", _C_Z), ("gate_b", _C_GATE_B),
                     ("gate_a", _C_GATE_A)):
        add_proj(name, wrest_ref, c0)
    add_proj("bd", wbd_ref, 0, width=128, n_blocks=1)
    mxu_order = ([(n, cb) for cb in blocks for n in ("glu_a", "glu_b")]
                 + [(n, cb) for n in ("q", "k", "v", "z", "gate_a", "gate_b") for cb in blocks]
                 + [("bd", 0)])
    mxu_jobs = [proj_jobs[key] for key in mxu_order]
    mxu_index = {key: i + 1 for i, key in enumerate(mxu_order)}

    def vpu(needs, job):
        return (max(mxu_index[n] for n in needs) if needs else 0, job)

    qbase, qgroups = _tap_groups(QKV_TAIL, SHORT_CONV)
    qkv_jobs = []
    for name, c0, out_ref in (("q", _C_Q, q_ref), ("k", _C_K, k_ref), ("v", _C_V, v_ref)):
        for cb in blocks:
            def job(c0=c0, out_ref=out_ref, cb=cb):
                cols = slice(c0 + cb * COL_BLOCK, c0 + (cb + 1) * COL_BLOCK)
                a = None
                for offs in qgroups:
                    part = _tap_group_sum(extq, cqkv_ref, cols, offs, qbase, tl)
                    a = part if a is None else a + part
                a = a.reshape(r, COL_BLOCK)
                a = a * _sigmoid(a)
                if out_ref is not v_ref:
                    scale = HEAD ** -0.5 if out_ref is q_ref else 1.0
                    heads = []
                    for h in range(COL_BLOCK // HEAD):
                        ah = a[:, h * HEAD:(h + 1) * HEAD]
                        ah = ah * lax.rsqrt(jnp.sum(ah * ah, axis=-1, keepdims=True) + EPS)
                        heads.append(ah * scale if scale != 1.0 else ah)
                    a = jnp.concatenate(heads, axis=-1)
                out_ref[:, :, cb_cols(cb)] = a.reshape(nb, tl, COL_BLOCK)
            qkv_jobs.append(vpu([(name, cb)], job))

    glu_jobs = []
    for cb in blocks:
        def job(cb=cb):
            glu = raw["glu_a", cb] * _sigmoid(raw["glu_b", cb])
            extc[:, CONF_TAIL:CONF_TAIL + tl, cb_cols(cb)] = glu.reshape(nb, tl, COL_BLOCK)
        glu_jobs.append(vpu([("glu_a", cb), ("glu_b", cb)], job))
    cbase, cgroups = _tap_groups(CONF_TAIL, CONF_KERNEL)
    conv = [None] * n_cb
    conv_jobs = []
    for cb in blocks:
        for offs in cgroups:
            def job(cb=cb, offs=offs):
                part = _tap_group_sum(extc, cdw_ref, cb_cols(cb), offs, cbase, tl)
                conv[cb] = part if conv[cb] is None else conv[cb] + part
            conv_jobs.append(vpu([], job))

    def gate_job(name, out_ref, silu, cb):
        def job():
            t = raw[name, cb]
            t = t * _sigmoid(t) if silu else _sigmoid(t)
            out_ref[:, :, cb_cols(cb)] = t.reshape(nb, tl, COL_BLOCK).astype(out_ref.dtype)
        return vpu([(name, cb)], job)

    def beta_decay():
        bd = raw["bd", 0]
        lane = lax.broadcasted_iota(jnp.int32, bd.shape, 1)
        xg = bd + decay_ref[1:2, :]
        softplus = jnp.maximum(xg, 0.0) + jnp.log1p(jnp.exp(-jnp.abs(xg)))
        g = -jnp.exp(decay_ref[0:1, :]) * softplus
        bg_ref[...] = jnp.where(lane < N_HEADS, _sigmoid(bd), g).reshape(nb, tl, 128)

    side_jobs = [gate_job(name, out_ref, silu, cb)
                 for name, out_ref, silu in (("z", zs_ref, True), ("gate_a", ga_ref, False))
                 for cb in blocks] + [vpu([("bd", 0)], beta_decay)]

    stats = {}

    def ln_stats():
        cs = [conv[cb].reshape(r, COL_BLOCK) + cvec_ref[0:1, cb_cols(cb)] for cb in blocks]
        mu = sum(jnp.sum(c, axis=-1, keepdims=True) for c in cs) * (1.0 / D_MODEL)
        xc = [c - mu for c in cs]
        var = sum(jnp.sum(v * v, axis=-1, keepdims=True) for v in xc) * (1.0 / D_MODEL)
        stats["xc"] = xc
        stats["rstd"] = lax.rsqrt(var + EPS)
    final_jobs = [vpu([], ln_stats)]
    for cb in blocks:
        def job(cb=cb):
            cols = cb_cols(cb)
            y = stats["xc"][cb] * stats["rstd"] * cvec_ref[1:2, cols] + cvec_ref[2:3, cols]
            obg = _sigmoid(raw["gate_b", cb]) * (y * _sigmoid(y))
            obg_ref[:, :, cols] = obg.reshape(nb, tl, COL_BLOCK).astype(obg_ref.dtype)
        final_jobs.append(vpu([("gate_b", cb)], job))

    vpu_jobs = [glu_jobs[0]]
    per_block = len(conv_jobs) // n_cb
    n_qkv = len(qkv_jobs)
    for i, cj in enumerate(conv_jobs):
        if i % per_block == 0 and i // per_block + 1 < n_cb:
            vpu_jobs.append(glu_jobs[i // per_block + 1])
        vpu_jobs.append(cj)
        while qkv_jobs and (n_qkv - len(qkv_jobs)) * len(conv_jobs) < n_qkv * (i + 1):
            vpu_jobs.append(qkv_jobs.pop(0))
    vpu_jobs += side_jobs + final_jobs
    _emit_interleaved(mxu_jobs, vpu_jobs, lead=2)

    qtail_ref[...] = extq[:, QKV_TAIL + tl - nq:QKV_TAIL + tl, :]
    extq[:, 0:QKV_TAIL, :] = extq[:, tl:tl + QKV_TAIL, :]
    ctail_ref[...] = extc[:, CONF_TAIL + tl - nc:CONF_TAIL + tl, :]
    extc[:, 0:CONF_TAIL, :] = extc[:, tl:tl + CONF_TAIL, :]


def _const_spec(shape):
    zeros = (0,) * len(shape)
    return pl.BlockSpec(shape, lambda *_: zeros, pipeline_mode=pl.Buffered(1))


def _proj_call(x, qkv0, conf0, wqkv, wrest, wbd, gpre, cqkv, decay, cdw, cvec, *, nb, tl):
    b, l, _ = x.shape
    grid = (b // nb, l // tl)
    seq = lambda width: pl.BlockSpec((nb, tl, width), lambda i, t: (i, t, 0))
    per_b = lambda rows, width: pl.BlockSpec((nb, rows, width), lambda i, t: (i, 0, 0))
    big = jax.ShapeDtypeStruct((b, l, D_MODEL), F32)
    gating = jax.ShapeDtypeStruct((b, l, D_MODEL), BF16)
    return pl.pallas_call(
        functools.partial(_proj_kernel, nb=nb, tl=tl),
        grid=grid,
        in_specs=[seq(D_MODEL), per_b(SHORT_CONV - 1, 3 * D_MODEL), per_b(CONF_KERNEL - 1, D_MODEL),
                  _const_spec(wqkv.shape), _const_spec(wrest.shape), _const_spec(wbd.shape),
                  _const_spec(gpre.shape), _const_spec(cqkv.shape),
                  _const_spec(decay.shape), _const_spec(cdw.shape), _const_spec(cvec.shape)],
        out_specs=[seq(D_MODEL), seq(D_MODEL), seq(D_MODEL), seq(128), seq(D_MODEL), seq(D_MODEL),
                   seq(D_MODEL), per_b(SHORT_CONV - 1, 3 * D_MODEL), per_b(CONF_KERNEL - 1, D_MODEL)],
        out_shape=[big, big, big, jax.ShapeDtypeStruct((b, l, 128), F32), gating, gating, gating,
                   jax.ShapeDtypeStruct((b, SHORT_CONV - 1, 3 * D_MODEL), F32),
                   jax.ShapeDtypeStruct((b, CONF_KERNEL - 1, D_MODEL), F32)],
        scratch_shapes=[pltpu.VMEM((nb, QKV_TAIL + tl, 3 * D_MODEL), F32),
                        pltpu.VMEM((nb, CONF_TAIL + tl, D_MODEL), F32)],
        compiler_params=pltpu.CompilerParams(
            dimension_semantics=("arbitrary", "arbitrary"), vmem_limit_bytes=VMEM_LIMIT_BYTES),
        name="proj",
    )(x, qkv0, conf0, wqkv, wrest, wbd, gpre, cqkv, decay, cdw, cvec)


def _delta_kernel(q_ref, k_ref, v_ref, bg_ref, s0_ref, o_ref, sout_ref, s_scr, *, nb, c):
    ci = pl.program_id(1)
    n_pairs = N_HEADS // 2
    pw = 2 * HEAD

    @pl.when(ci == 0)
    def _():
        for bi in range(nb):
            for p in range(n_pairs):
                s_scr[bi, p] = jnp.concatenate([s0_ref[bi, 2 * p], s0_ref[bi, 2 * p + 1]], axis=1)

    iota = lambda shape, d: lax.broadcasted_iota(jnp.int32, shape, d)
    row = iota((c, 2 * c), 0)
    lane = iota((c, 2 * c), 1)
    col = lane % c
    first = lane < c
    causal = row >= col
    strict = row > col
    ident = (row == col).astype(F32)
    first_w = iota((c, pw), 1) < HEAD
    tri = (iota((c, c), 0) >= iota((c, c), 1)).astype(F32)
    sel = (iota((n_pairs, 128), 1) // 2 == iota((n_pairs, 128), 0) + N_HEADS // 2).astype(F32)
    odd_lane = iota((c, 128), 1) % 2 == 1

    def bdiag(x, split):
        return jnp.concatenate([jnp.where(split, x, 0.0), jnp.where(split, 0.0, x)], axis=0)

    chains = [(bi, p) for bi in range(nb) for p in range(n_pairs)]
    each = lambda f, *lists: [f(*args) for args in zip(*lists)]
    psl = lambda p: slice(p * pw, (p + 1) * pw)
    pick = lambda cols, p: jnp.where(first, cols[:, 2 * p:2 * p + 1], cols[:, 2 * p + 1:2 * p + 2])
    pick_w = lambda cols, p: jnp.where(first_w[:cols.shape[0]], cols[:, 2 * p:2 * p + 1],
                                       cols[:, 2 * p + 1:2 * p + 2])

    bgs = [bg_ref[bi] for bi in range(nb)]
    gams = [_mm(tri, bg, exact=True) for bg in bgs]
    gam_ts = [_mm(sel, jnp.concatenate([jnp.where(odd_lane, 0.0, g), jnp.where(odd_lane, g, 0.0)], axis=0),
                  _NT, exact=True) for g in gams]
    gcols = [g[:, N_HEADS:2 * N_HEADS] for g in gams]
    q = [q_ref[bi, :, psl(p)] for bi, p in chains]
    k = [k_ref[bi, :, psl(p)] for bi, p in chains]
    v = [v_ref[bi, :, psl(p)] for bi, p in chains]
    beta = [pick(bgs[bi], p) for bi, p in chains]
    beta_w = [pick_w(bgs[bi], p) for bi, p in chains]
    gc = [pick(gcols[bi], p) for bi, p in chains]
    gc_w = [pick_w(gcols[bi], p) for bi, p in chains]
    gl_w = [pick_w(gcols[bi][c - 1:c, :], p) for bi, p in chains]
    gr = [gam_ts[bi][p:p + 1, :] for bi, p in chains]
    decay = each(lambda a, b: jnp.where(causal, jnp.exp(jnp.where(causal, a - b, 0.0)), 0.0), gc, gr)
    kb = each(lambda a: a.astype(BF16), k)
    kkqk = each(lambda a, b: _mm(jnp.concatenate([a, b.astype(BF16)], axis=0), bdiag(a, first_w), _NT),
                kb, q)
    n = each(lambda b, a, d: jnp.where(strict, -(b * a[:c] * d), 0.0), beta, kkqk, decay)
    eg_w = each(jnp.exp, gc_w)
    t = each(lambda a: ident + a, n)
    pw_ = each(lambda a: _mm(a, bdiag(a, first)), n)
    span = 4
    while span <= c:
        pbd = each(lambda a: bdiag(a, first), pw_)
        if span < c:
            both = each(lambda a, b, m: _mm(jnp.concatenate([a, b], axis=0), m), t, pw_, pbd)
            t = each(lambda a, r: a + r[:c], t, both)
            pw_ = each(lambda r: r[c:], both)
        else:
            t = each(lambda a, m: a + _mm(a, m), t, pbd)
        span *= 2
    rhs = each(lambda b, vv, e, kk_: jnp.concatenate(
        [jnp.concatenate([b[:, :HEAD] * vv[:, :HEAD], (b * e * kk_)[:, :HEAD]], axis=1),
         jnp.concatenate([b[:, HEAD:] * vv[:, HEAD:], (b * e * kk_)[:, HEAD:]], axis=1)], axis=0),
        beta_w, v, eg_w, k)
    sol = each(lambda a, r: _mm(bdiag(a, first), r), t, rhs)
    s = [s_scr[bi, p] for bi, p in chains]
    first_s = iota((HEAD, pw), 1) < HEAD
    sbd = each(lambda a: bdiag(a, first_s).astype(BF16), s)
    kd = each(lambda x: jnp.concatenate([x[:c, HEAD:], x[c:, HEAD:]], axis=1), sol)
    vw = each(lambda x: jnp.concatenate([x[:c, :HEAD], x[c:, :HEAD]], axis=1), sol)
    ks_qs = each(lambda a, qq, e, m: _mm(jnp.concatenate([a, qq * e], axis=0), m), kd, q, eg_w, sbd)
    u = each(lambda a, r: a - r[:c], vw, ks_qs)
    ubd = each(lambda a: bdiag(a, first_w).astype(BF16), u)
    o = each(lambda r, a, d, m: r[c:] + _mm(a[c:] * d, m), ks_qs, kkqk, decay, ubd)
    ko = each(lambda kk_, gl, g: kk_ * jnp.exp(gl - g), k, gl_w, gc_w)
    s_new = each(lambda gl, a, kk_, m: jnp.exp(gl) * a + _mm(
        jnp.concatenate([kk_[:, :HEAD], kk_[:, HEAD:]], axis=0), m, _TN), gl_w, s, ko, ubd)
    for (bi, p), o_p, s_p in zip(chains, o, s_new):
        o_ref[bi, :, psl(p)] = o_p
        s_scr[bi, p] = s_p

    @pl.when(ci == pl.num_programs(1) - 1)
    def _():
        for bi in range(nb):
            for p in range(n_pairs):
                sout_ref[bi, 2 * p] = s_scr[bi, p, :, 0:HEAD]
                sout_ref[bi, 2 * p + 1] = s_scr[bi, p, :, HEAD:pw]


def _delta_call(q, k, v, bg, s0, *, nb, c):
    b, l, _ = q.shape
    seq = lambda width: pl.BlockSpec((nb, c, width), lambda i, t: (i, t, 0))
    st = pl.BlockSpec((nb, N_HEADS, HEAD, HEAD), lambda i, t: (i, 0, 0, 0))
    return pl.pallas_call(
        functools.partial(_delta_kernel, nb=nb, c=c),
        grid=(b // nb, l // c),
        in_specs=[seq(D_MODEL), seq(D_MODEL), seq(D_MODEL), seq(128), st],
        out_specs=[seq(D_MODEL), st],
        out_shape=[jax.ShapeDtypeStruct((b, l, D_MODEL), F32),
                   jax.ShapeDtypeStruct((b, N_HEADS, HEAD, HEAD), F32)],
        scratch_shapes=[pltpu.VMEM((nb, N_HEADS // 2, HEAD, 2 * HEAD), F32)],
        compiler_params=pltpu.CompilerParams(
            dimension_semantics=("arbitrary", "arbitrary"), vmem_limit_bytes=VMEM_LIMIT_BYTES),
        name="delta",
    )(q, k, v, bg, s0)


FF_SPLIT = 4


def _out_kernel(x_ref, o_ref, zs_ref, ga_ref, obg_ref, wout_ref, wup_ref, wdn_ref, vec_ref, y_ref):
    o = o_ref[...]
    parts = []
    for h in range(N_HEADS):
        oh = o[:, h * HEAD:(h + 1) * HEAD]
        parts.append(oh * lax.rsqrt(jnp.mean(oh * oh, axis=-1, keepdims=True) + EPS))
    oa = jnp.concatenate(parts, axis=-1) * vec_ref[0:1, :] * zs_ref[...].astype(F32)
    mixed = ga_ref[...].astype(F32) * oa + obg_ref[...].astype(F32)
    a = jnp.dot(mixed.astype(BF16), wout_ref[...], preferred_element_type=F32)
    x1 = x_ref[...] + _rms(a, vec_ref[1:2, :])
    hn = _rms(x1, vec_ref[2:3, :]).astype(BF16)
    fw = D_FF // FF_SPLIT
    acc = None
    for i in range(FF_SPLIT):
        hf = jnp.dot(hn, wup_ref[:, i * fw:(i + 1) * fw], preferred_element_type=F32)
        f = jnp.square(jnp.maximum(hf, 0.0)).astype(BF16)
        d = jnp.dot(f, wdn_ref[i * fw:(i + 1) * fw, :], preferred_element_type=F32)
        acc = d if acc is None else acc + d
    y_ref[...] = x1 + _rms(acc, vec_ref[3:4, :])


def _out_call(x, o, zs, ga, obg, wout, wup, wdn, vec, *, tm):
    n = x.shape[0]
    tok = pl.BlockSpec((tm, D_MODEL), lambda i: (i, 0))
    return pl.pallas_call(
        _out_kernel,
        grid=(n // tm,),
        in_specs=[tok, tok, tok, tok, tok, _const_spec(wout.shape), _const_spec(wup.shape),
                  _const_spec(wdn.shape), _const_spec(vec.shape)],
        out_specs=tok,
        out_shape=jax.ShapeDtypeStruct((n, D_MODEL), F32),
        compiler_params=pltpu.CompilerParams(
            dimension_semantics=("arbitrary",), vmem_limit_bytes=VMEM_LIMIT_BYTES),
        name="out",
    )(x, o, zs, ga, obg, wout, wup, wdn, vec)


def _tiles(b, l):
    if l >= 256:
        proj = (1, 256)
    else:
        proj = (min(b, 128 // l), l)
    delta = (min(b, 8), min(l, CHUNK))
    tm = min(b * l, 512)
    return dict(nb=proj[0], tl=proj[1], nbd=delta[0], c=delta[1], tm=tm)


def _layer(x, s0, qkv0, conf0, weights):
    wqkv, wrest, wbd, gpre, cqkv, decay, cdw, cvec, wout, wup, wdn, vec = weights
    b, l, _ = x.shape
    t = _tiles(b, l)
    q, k, v, bg, zs, ga, obg, qtail, ctail = _proj_call(
        x, qkv0, conf0, wqkv, wrest, wbd, gpre, cqkv, decay, cdw, cvec, nb=t["nb"], tl=t["tl"])
    o, s_new = _delta_call(q, k, v, bg, s0, nb=t["nbd"], c=t["c"])
    flat = lambda a: a.reshape(b * l, D_MODEL)
    y = _out_call(flat(x), flat(o), flat(zs), flat(ga), flat(obg), wout, wup, wdn, vec, tm=t["tm"])
    return y.reshape(b, l, D_MODEL), s_new, qtail, ctail


def _prep_weights(w_in, conv_qkv_w, a_log, dt_bias, delta_norm_g, conf_dw_w, conf_dw_b, conf_ln_g,
                  conf_ln_b, w_out, g_pre_mix, g_post_mix, g_pre_ffn, g_post_ffn, w_up, w_down):
    qkv_w = 3 * D_MODEL
    o_bd = qkv_w
    o_z = o_bd + 2 * N_HEADS
    wqkv = w_in[:, :qkv_w].astype(BF16)
    wrest = w_in[:, o_z:].astype(BF16)
    wbd = jnp.pad(w_in[:, o_bd:o_z], ((0, 0), (0, 128 - 2 * N_HEADS))).astype(BF16)
    pad_heads = lambda t: jnp.pad(t, (N_HEADS, 128 - 2 * N_HEADS))
    decay = jnp.stack([pad_heads(a_log), pad_heads(dt_bias)]).astype(F32)
    cvec = jnp.stack([conf_dw_b, conf_ln_g, conf_ln_b])
    vec = jnp.stack([jnp.tile(delta_norm_g, N_HEADS), g_post_mix, g_pre_ffn, g_post_ffn])
    return (wqkv, wrest, wbd, g_pre_mix[None, :], conv_qkv_w, decay, conf_dw_w, cvec,
            w_out.astype(BF16), w_up.astype(BF16), w_down.astype(BF16), vec)


def kernel(x_prompt, x_sample, state_delta, state_qkv_conv, state_conf_conv, w_in, conv_qkv_w, a_log,
           dt_bias, delta_norm_g, conf_dw_w, conf_dw_b, conf_ln_g, conf_ln_b, w_out, g_pre_mix,
           g_post_mix, g_pre_ffn, g_post_ffn, w_up, w_down):
    depth = w_in.shape[0]
    bp = x_prompt.shape[0]
    xp, xs = x_prompt, x_sample
    outs = [[] for _ in range(6)]
    for l in range(depth):
        weights = _prep_weights(
            w_in[l], conv_qkv_w[l], a_log[l], dt_bias[l], delta_norm_g[l], conf_dw_w[l], conf_dw_b[l],
            conf_ln_g[l], conf_ln_b[l], w_out[l], g_pre_mix[l], g_post_mix[l], g_pre_ffn[l],
            g_post_ffn[l], w_up[l], w_down[l])
        xp, s_p, bq_p, bc_p = _layer(
            xp, jnp.zeros((bp, N_HEADS, HEAD, HEAD), state_delta.dtype),
            jnp.zeros((bp, SHORT_CONV - 1, 3 * D_MODEL), F32),
            jnp.zeros((bp, CONF_KERNEL - 1, D_MODEL), F32), weights)
        xs, s_s, bq_s, bc_s = _layer(
            xs, state_delta[l], state_qkv_conv[l], state_conf_conv[l], weights)
        for lst, val in zip(outs, (s_p, bq_p, bc_p, s_s, bq_s, bc_s)):
            lst.append(val)
    return (xp, xs) + tuple(jnp.stack(o) for o in outs)
```

```python
import functools

import jax
import jax.numpy as jnp
from jax import lax
from jax.experimental import pallas as pl
from jax.experimental.pallas import tpu as pltpu

D_MODEL = 1024
N_HEADS = 8
HEAD = 128
SHORT_CONV = 4
CONF_KERNEL = 31
D_FF = 4 * D_MODEL
EPS = 1e-6
CHUNK = 64

QKV_TAIL = 8
CONF_TAIL = 32

_C_Q, _C_K, _C_V = 0, D_MODEL, 2 * D_MODEL
_C_Z, _C_GLU_A, _C_GLU_B, _C_GATE_A, _C_GATE_B = (i * D_MODEL for i in range(5))

VMEM_LIMIT_BYTES = 56 * 1024 * 1024

F32 = jnp.float32
BF16 = jnp.bfloat16
HI = lax.Precision.HIGHEST

_NN = (((1,), (0,)), ((), ()))
_NT = (((1,), (1,)), ((), ()))
_TN = (((0,), (0,)), ((), ()))


def _mm(a, b, dims=_NN, exact=False):
    if exact:
        return lax.dot_general(a, b, dims, precision=HI, preferred_element_type=F32)
    return lax.dot_general(a.astype(BF16), b.astype(BF16), dims, preferred_element_type=F32)


def _sigmoid(x):
    return 1.0 / (1.0 + jnp.exp(-x))


def _rms(x, g):
    return x * lax.rsqrt(jnp.mean(x * x, axis=-1, keepdims=True) + EPS) * g


COL_BLOCK = 256


def _tap_groups(tail, taps):
    base = tail - (taps - 1)
    groups = [[o for o in range(base, tail + 1) if o % 8 == res] for res in range(8)]
    return base, [g for g in groups if g]


def _tap_group_sum(ext_ref, w_ref, cols, offs, base, tl):
    res = offs[0] % 8
    rows = tl + 8 if res else tl
    part = None
    for o in offs:
        term = ext_ref[:, o - res:o - res + rows, cols] * w_ref[o - base:o - base + 1, cols]
        part = term if part is None else part + term
    if res:
        nb, _, width = part.shape
        sub = lax.broadcasted_iota(jnp.int32, (nb, tl, width), 1) % 8
        merged = jnp.where(sub >= res, part[:, :tl, :], part[:, 8:, :])
        merged = merged.reshape(nb, tl // 8, 8, width)
        part = pltpu.roll(merged, 8 - res, axis=2).reshape(nb, tl, width)
    return part


def _emit_interleaved(mxu_jobs, vpu_jobs, lead):
    done = 0
    n_m, n_v = len(mxu_jobs), len(vpu_jobs)
    for i, (need, job) in enumerate(vpu_jobs):
        target = min(n_m, max(need, lead + (i * n_m) // n_v))
        while done < target:
            mxu_jobs[done]()
            done += 1
        job()
    while done < n_m:
        mxu_jobs[done]()
        done += 1


def _proj_kernel(x_ref, qkv0_ref, conf0_ref, wqkv_ref, wrest_ref, wbd_ref, gpre_ref, cqkv_ref,
                 decay_ref, cdw_ref, cvec_ref,
                 q_ref, k_ref, v_ref, bg_ref, zs_ref, ga_ref, obg_ref, qtail_ref, ctail_ref,
                 extq, extc, *, nb, tl):
    r = nb * tl
    nq = SHORT_CONV - 1
    nc = CONF_KERNEL - 1
    n_cb = D_MODEL // COL_BLOCK
    blocks = range(n_cb)
    cb_cols = lambda cb: slice(cb * COL_BLOCK, (cb + 1) * COL_BLOCK)

    @pl.when(pl.program_id(1) == 0)
    def _():
        extq[:, 0:QKV_TAIL, :] = jnp.zeros((nb, QKV_TAIL, 3 * D_MODEL), F32)
        for j in range(nq):
            extq[:, QKV_TAIL - nq + j, :] = qkv0_ref[j, 0]
        extc[:, 0:8, :] = jnp.zeros((nb, 8, D_MODEL), F32)
        for j in range(nc):
            extc[:, CONF_TAIL - nc + j, :] = conf0_ref[j, 0]

    x = x_ref[...].reshape(r, D_MODEL)
    hb = _rms(x, gpre_ref[...]).astype(BF16)
    raw = {}
    proj_jobs = {}

    def add_proj(name, w_ref, c0, width=COL_BLOCK, n_blocks=n_cb, into_extq=False):
        for cb in range(n_blocks):
            def job(cb=cb):
                lo = c0 + cb * width
                val = jnp.dot(hb, w_ref[:, lo:lo + width], preferred_element_type=F32)
                if into_extq:
                    extq[:, QKV_TAIL:QKV_TAIL + tl, lo:lo + width] = val.reshape(nb, tl, width)
                else:
                    raw[name, cb] = val
            proj_jobs[name, cb] = job

    for name, c0 in (("q", _C_Q), ("k", _C_K), ("v", _C_V)):
        add_proj(name, wqkv_ref, c0, into_extq=True)
    for name, c0 in (("glu_a", _C_GLU_A), ("glu_b", _C_GLU_B), ("z", _C_Z), ("gate_b", _C_GATE_B),
                     ("gate_a", _C_GATE_A)):
        add_proj(name, wrest_ref, c0)
    add_proj("bd", wbd_ref, 0, width=128, n_blocks=1)
    mxu_order = ([(n, cb) for cb in blocks for n in ("glu_a", "glu_b")]
                 + [(n, cb) for n in ("q", "k", "v", "z", "gate_a", "gate_b") for cb in blocks]
                 + [("bd", 0)])
    mxu_jobs = [proj_jobs[key] for key in mxu_order]
    mxu_index = {key: i + 1 for i, key in enumerate(mxu_order)}

    def vpu(needs, job):
        return (max(mxu_index[n] for n in needs) if needs else 0, job)

    qbase, qgroups = _tap_groups(QKV_TAIL, SHORT_CONV)
    qkv_jobs = []
    for name, c0, out_ref in (("q", _C_Q, q_ref), ("k", _C_K, k_ref), ("v", _C_V, v_ref)):
        for cb in blocks:
            def job(c0=c0, out_ref=out_ref, cb=cb):
                cols = slice(c0 + cb * COL_BLOCK, c0 + (cb + 1) * COL_BLOCK)
                a = None
                for offs in qgroups:
                    part = _tap_group_sum(extq, cqkv_ref, cols, offs, qbase, tl)
                    a = part if a is None else a + part
                a = a.reshape(r, COL_BLOCK)
                a = a * _sigmoid(a)
                if out_ref is not v_ref:
                    scale = HEAD ** -0.5 if out_ref is q_ref else 1.0
                    heads = []
                    for h in range(COL_BLOCK // HEAD):
                        ah = a[:, h * HEAD:(h + 1) * HEAD]
                        ah = ah * lax.rsqrt(jnp.sum(ah * ah, axis=-1, keepdims=True) + EPS)
                        heads.append(ah * scale if scale != 1.0 else ah)
                    a = jnp.concatenate(heads, axis=-1)
                out_ref[:, :, cb_cols(cb)] = a.reshape(nb, tl, COL_BLOCK)
            qkv_jobs.append(vpu([(name, cb)], job))

    glu_jobs = []
    for cb in blocks:
        def job(cb=cb):
            glu = raw["glu_a", cb] * _sigmoid(raw["glu_b", cb])
            extc[:, CONF_TAIL:CONF_TAIL + tl, cb_cols(cb)] = glu.reshape(nb, tl, COL_BLOCK)
        glu_jobs.append(vpu([("glu_a", cb), ("glu_b", cb)], job))
    cbase, cgroups = _tap_groups(CONF_TAIL, CONF_KERNEL)
    conv = [None] * n_cb
    conv_jobs = []
    for cb in blocks:
        for offs in cgroups:
            def job(cb=cb, offs=offs):
                part = _tap_group_sum(extc, cdw_ref, cb_cols(cb), offs, cbase, tl)
                conv[cb] = part if conv[cb] is None else conv[cb] + part
            conv_jobs.append(vpu([], job))

    def gate_job(name, out_ref, silu, cb):
        def job():
            t = raw[name, cb]
            t = t * _sigmoid(t) if silu else _sigmoid(t)
            out_ref[:, :, cb_cols(cb)] = t.reshape(nb, tl, COL_BLOCK)
        return vpu([(name, cb)], job)

    def beta_decay():
        bd = raw["bd", 0]
        lane = lax.broadcasted_iota(jnp.int32, bd.shape, 1)
        xg = bd + decay_ref[1:2, :]
        softplus = jnp.maximum(xg, 0.0) + jnp.log1p(jnp.exp(-jnp.abs(xg)))
        g = -jnp.exp(decay_ref[0:1, :]) * softplus
        bg_ref[...] = jnp.where(lane < N_HEADS, _sigmoid(bd), g).reshape(nb, tl, 128)

    side_jobs = [gate_job(name, out_ref, silu, cb)
                 for name, out_ref, silu in (("z", zs_ref, True), ("gate_a", ga_ref, False))
                 for cb in blocks] + [vpu([("bd", 0)], beta_decay)]

    stats = {}

    def ln_stats():
        cs = [conv[cb].reshape(r, COL_BLOCK) + cvec_ref[0:1, cb_cols(cb)] for cb in blocks]
        mu = sum(jnp.sum(c, axis=-1, keepdims=True) for c in cs) * (1.0 / D_MODEL)
        xc = [c - mu for c in cs]
        var = sum(jnp.sum(v * v, axis=-1, keepdims=True) for v in xc) * (1.0 / D_MODEL)
        stats["xc"] = xc
        stats["rstd"] = lax.rsqrt(var + EPS)
    final_jobs = [vpu([], ln_stats)]
    for cb in blocks:
        def job(cb=cb):
            cols = cb_cols(cb)
            y = stats["xc"][cb] * stats["rstd"] * cvec_ref[1:2, cols] + cvec_ref[2:3, cols]
            obg = _sigmoid(raw["gate_b", cb]) * (y * _sigmoid(y))
            obg_ref[:, :, cols] = obg.reshape(nb, tl, COL_BLOCK)
        final_jobs.append(vpu([("gate_b", cb)], job))

    vpu_jobs = [glu_jobs[0]]
    per_block = len(conv_jobs) // n_cb
    n_qkv = len(qkv_jobs)
    for i, cj in enumerate(conv_jobs):
        if i % per_block == 0 and i // per_block + 1 < n_cb:
            vpu_jobs.append(glu_jobs[i // per_block + 1])
        vpu_jobs.append(cj)
        while qkv_jobs and (n_qkv - len(qkv_jobs)) * len(conv_jobs) < n_qkv * (i + 1):
            vpu_jobs.append(qkv_jobs.pop(0))
    vpu_jobs += side_jobs + final_jobs
    _emit_interleaved(mxu_jobs, vpu_jobs, lead=2)

    for j in range(nq):
        qtail_ref[j, 0] = extq[:, QKV_TAIL + tl - nq + j, :]
    extq[:, 0:QKV_TAIL, :] = extq[:, tl:tl + QKV_TAIL, :]
    for j in range(nc):
        ctail_ref[j, 0] = extc[:, CONF_TAIL + tl - nc + j, :]
    extc[:, 0:CONF_TAIL, :] = extc[:, tl:tl + CONF_TAIL, :]


def _const_spec(shape):
    zeros = (0,) * len(shape)
    return pl.BlockSpec(shape, lambda *_: zeros, pipeline_mode=pl.Buffered(1))


def _proj_call(x, qkv0, conf0, wqkv, wrest, wbd, gpre, cqkv, decay, cdw, cvec, *, nb, tl):
    b, l, _ = x.shape
    grid = (b // nb, l // tl)
    seq = lambda width: pl.BlockSpec((nb, tl, width), lambda i, t: (i, t, 0))
    per_b = lambda rows, width: pl.BlockSpec((rows, 1, nb, width), lambda i, t: (0, i, 0, 0))
    big = jax.ShapeDtypeStruct((b, l, D_MODEL), F32)
    return pl.pallas_call(
        functools.partial(_proj_kernel, nb=nb, tl=tl),
        grid=grid,
        in_specs=[seq(D_MODEL), per_b(SHORT_CONV - 1, 3 * D_MODEL), per_b(CONF_KERNEL - 1, D_MODEL),
                  _const_spec(wqkv.shape), _const_spec(wrest.shape), _const_spec(wbd.shape),
                  _const_spec(gpre.shape), _const_spec(cqkv.shape),
                  _const_spec(decay.shape), _const_spec(cdw.shape), _const_spec(cvec.shape)],
        out_specs=[seq(D_MODEL), seq(D_MODEL), seq(D_MODEL), seq(128), seq(D_MODEL), seq(D_MODEL),
                   seq(D_MODEL), per_b(SHORT_CONV - 1, 3 * D_MODEL), per_b(CONF_KERNEL - 1, D_MODEL)],
        out_shape=[big, big, big, jax.ShapeDtypeStruct((b, l, 128), F32), big, big, big,
                   jax.ShapeDtypeStruct((SHORT_CONV - 1, b // nb, nb, 3 * D_MODEL), F32),
                   jax.ShapeDtypeStruct((CONF_KERNEL - 1, b // nb, nb, D_MODEL), F32)],
        scratch_shapes=[pltpu.VMEM((nb, QKV_TAIL + tl, 3 * D_MODEL), F32),
                        pltpu.VMEM((nb, CONF_TAIL + tl, D_MODEL), F32)],
        compiler_params=pltpu.CompilerParams(
            dimension_semantics=("arbitrary", "arbitrary"), vmem_limit_bytes=VMEM_LIMIT_BYTES),
        name="proj",
    )(x, qkv0, conf0, wqkv, wrest, wbd, gpre, cqkv, decay, cdw, cvec)


def _delta_kernel(q_ref, k_ref, v_ref, bg_ref, s0_ref, o_ref, sout_ref, s_scr, *, nb, c):
    ci = pl.program_id(1)
    n_pairs = N_HEADS // 2
    pw = 2 * HEAD

    @pl.when(ci == 0)
    def _():
        for bi in range(nb):
            for p in range(n_pairs):
                s_scr[bi, p] = jnp.concatenate([s0_ref[bi, 2 * p], s0_ref[bi, 2 * p + 1]], axis=1)

    iota = lambda shape, d: lax.broadcasted_iota(jnp.int32, shape, d)
    row = iota((c, 2 * c), 0)
    lane = iota((c, 2 * c), 1)
    col = lane % c
    first = lane < c
    causal = row >= col
    strict = row > col
    ident = (row == col).astype(F32)
    first_w = iota((c, pw), 1) < HEAD
    tri = (iota((c, c), 0) >= iota((c, c), 1)).astype(F32)
    sel = (iota((n_pairs, 128), 1) // 2 == iota((n_pairs, 128), 0) + N_HEADS // 2).astype(F32)
    odd_lane = iota((c, 128), 1) % 2 == 1

    def bdiag(x, split):
        return jnp.concatenate([jnp.where(split, x, 0.0), jnp.where(split, 0.0, x)], axis=0)

    chains = [(bi, p) for bi in range(nb) for p in range(n_pairs)]
    each = lambda f, *lists: [f(*args) for args in zip(*lists)]
    psl = lambda p: slice(p * pw, (p + 1) * pw)
    pick = lambda cols, p: jnp.where(first, cols[:, 2 * p:2 * p + 1], cols[:, 2 * p + 1:2 * p + 2])
    pick_w = lambda cols, p: jnp.where(first_w[:cols.shape[0]], cols[:, 2 * p:2 * p + 1],
                                       cols[:, 2 * p + 1:2 * p + 2])

    bgs = [bg_ref[bi] for bi in range(nb)]
    gams = [_mm(tri, bg, exact=True) for bg in bgs]
    gam_ts = [_mm(sel, jnp.concatenate([jnp.where(odd_lane, 0.0, g), jnp.where(odd_lane, g, 0.0)], axis=0),
                  _NT, exact=True) for g in gams]
    gcols = [g[:, N_HEADS:2 * N_HEADS] for g in gams]
    q = [q_ref[bi, :, psl(p)] for bi, p in chains]
    k = [k_ref[bi, :, psl(p)] for bi, p in chains]
    v = [v_ref[bi, :, psl(p)] for bi, p in chains]
    beta = [pick(bgs[bi], p) for bi, p in chains]
    beta_w = [pick_w(bgs[bi], p) for bi, p in chains]
    gc = [pick(gcols[bi], p) for bi, p in chains]
    gc_w = [pick_w(gcols[bi], p) for bi, p in chains]
    gl_w = [pick_w(gcols[bi][c - 1:c, :], p) for bi, p in chains]
    gr = [gam_ts[bi][p:p + 1, :] for bi, p in chains]
    decay = each(lambda a, b: jnp.where(causal, jnp.exp(jnp.where(causal, a - b, 0.0)), 0.0), gc, gr)
    kb = each(lambda a: a.astype(BF16), k)
    kkqk = each(lambda a, b: _mm(jnp.concatenate([a, b.astype(BF16)], axis=0), bdiag(a, first_w), _NT),
                kb, q)
    n = each(lambda b, a, d: jnp.where(strict, -(b * a[:c] * d), 0.0), beta, kkqk, decay)
    eg_w = each(jnp.exp, gc_w)
    t = each(lambda a: ident + a, n)
    pw_ = each(lambda a: _mm(a, bdiag(a, first)), n)
    span = 4
    while span <= c:
        pbd = each(lambda a: bdiag(a, first), pw_)
        if span < c:
            both = each(lambda a, b, m: _mm(jnp.concatenate([a, b], axis=0), m), t, pw_, pbd)
            t = each(lambda a, r: a + r[:c], t, both)
            pw_ = each(lambda r: r[c:], both)
        else:
            t = each(lambda a, m: a + _mm(a, m), t, pbd)
        span *= 2
    rhs = each(lambda b, vv, e, kk_: jnp.concatenate(
        [jnp.concatenate([b[:, :HEAD] * vv[:, :HEAD], (b * e * kk_)[:, :HEAD]], axis=1),
         jnp.concatenate([b[:, HEAD:] * vv[:, HEAD:], (b * e * kk_)[:, HEAD:]], axis=1)], axis=0),
        beta_w, v, eg_w, k)
    sol = each(lambda a, r: _mm(bdiag(a, first), r), t, rhs)
    s = [s_scr[bi, p] for bi, p in chains]
    first_s = iota((HEAD, pw), 1) < HEAD
    sbd = each(lambda a: bdiag(a, first_s).astype(BF16), s)
    kd = each(lambda x: jnp.concatenate([x[:c, HEAD:], x[c:, HEAD:]], axis=1), sol)
    vw = each(lambda x: jnp.concatenate([x[:c, :HEAD], x[c:, :HEAD]], axis=1), sol)
    ks_qs = each(lambda a, qq, e, m: _mm(jnp.concatenate([a, qq * e], axis=0), m), kd, q, eg_w, sbd)
    u = each(lambda a, r: a - r[:c], vw, ks_qs)
    ubd = each(lambda a: bdiag(a, first_w).astype(BF16), u)
    o = each(lambda r, a, d, m: r[c:] + _mm(a[c:] * d, m), ks_qs, kkqk, decay, ubd)
    ko = each(lambda kk_, gl, g: kk_ * jnp.exp(gl - g), k, gl_w, gc_w)
    s_new = each(lambda gl, a, kk_, m: jnp.exp(gl) * a + _mm(
        jnp.concatenate([kk_[:, :HEAD], kk_[:, HEAD:]], axis=0), m, _TN), gl_w, s, ko, ubd)
    for (bi, p), o_p, s_p in zip(chains, o, s_new):
        o_ref[bi, :, psl(p)] = o_p
        s_scr[bi, p] = s_p

    @pl.when(ci == pl.num_programs(1) - 1)
    def _():
        for bi in range(nb):
            for p in range(n_pairs):
                sout_ref[bi, 2 * p] = s_scr[bi, p, :, 0:HEAD]
                sout_ref[bi, 2 * p + 1] = s_scr[bi, p, :, HEAD:pw]


def _delta_call(q, k, v, bg, s0, *, nb, c):
    b, l, _ = q.shape
    seq = lambda width: pl.BlockSpec((nb, c, width), lambda i, t: (i, t, 0))
    st = pl.BlockSpec((nb, N_HEADS, HEAD, HEAD), lambda i, t: (i, 0, 0, 0))
    return pl.pallas_call(
        functools.partial(_delta_kernel, nb=nb, c=c),
        grid=(b // nb, l // c),
        in_specs=[seq(D_MODEL), seq(D_MODEL), seq(D_MODEL), seq(128), st],
        out_specs=[seq(D_MODEL), st],
        out_shape=[jax.ShapeDtypeStruct((b, l, D_MODEL), F32),
                   jax.ShapeDtypeStruct((b, N_HEADS, HEAD, HEAD), F32)],
        scratch_shapes=[pltpu.VMEM((nb, N_HEADS // 2, HEAD, 2 * HEAD), F32)],
        compiler_params=pltpu.CompilerParams(
            dimension_semantics=("arbitrary", "arbitrary"), vmem_limit_bytes=VMEM_LIMIT_BYTES),
        name="delta",
    )(q, k, v, bg, s0)


FF_SPLIT = 4


def _out_kernel(x_ref, o_ref, zs_ref, ga_ref, obg_ref, wout_ref, wup_ref, wdn_ref, vec_ref, y_ref):
    o = o_ref[...]
    parts = []
    for h in range(N_HEADS):
        oh = o[:, h * HEAD:(h + 1) * HEAD]
        parts.append(oh * lax.rsqrt(jnp.mean(oh * oh, axis=-1, keepdims=True) + EPS))
    oa = jnp.concatenate(parts, axis=-1) * vec_ref[0:1, :] * zs_ref[...]
    mixed = ga_ref[...] * oa + obg_ref[...]
    a = jnp.dot(mixed.astype(BF16), wout_ref[...], preferred_element_type=F32)
    x1 = x_ref[...] + _rms(a, vec_ref[1:2, :])
    hn = _rms(x1, vec_ref[2:3, :]).astype(BF16)
    fw = D_FF // FF_SPLIT
    acc = None
    for i in range(FF_SPLIT):
        hf = jnp.dot(hn, wup_ref[:, i * fw:(i + 1) * fw], preferred_element_type=F32)
        f = jnp.square(jnp.maximum(hf, 0.0)).astype(BF16)
        d = jnp.dot(f, wdn_ref[i * fw:(i + 1) * fw, :], preferred_element_type=F32)
        acc = d if acc is None else acc + d
    y_ref[...] = x1 + _rms(acc, vec_ref[3:4, :])


def _out_call(x, o, zs, ga, obg, wout, wup, wdn, vec, *, tm):
    n = x.shape[0]
    tok = pl.BlockSpec((tm, D_MODEL), lambda i: (i, 0))
    return pl.pallas_call(
        _out_kernel,
        grid=(n // tm,),
        in_specs=[tok, tok, tok, tok, tok, _const_spec(wout.shape), _const_spec(wup.shape),
                  _const_spec(wdn.shape), _const_spec(vec.shape)],
        out_specs=tok,
        out_shape=jax.ShapeDtypeStruct((n, D_MODEL), F32),
        compiler_params=pltpu.CompilerParams(
            dimension_semantics=("arbitrary",), vmem_limit_bytes=VMEM_LIMIT_BYTES),
        name="out",
    )(x, o, zs, ga, obg, wout, wup, wdn, vec)


def _tiles(b, l):
    if l >= 256:
        proj = (1, 256)
    else:
        proj = (min(b, 128 // l), l)
    delta = (min(b, 8), min(l, CHUNK))
    tm = min(b * l, 512)
    return dict(nb=proj[0], tl=proj[1], nbd=delta[0], c=delta[1], tm=tm)


def _layer(x, s0, qkv0, conf0, weights):
    wqkv, wrest, wbd, gpre, cqkv, decay, cdw, cvec, wout, wup, wdn, vec = weights
    b, l, _ = x.shape
    t = _tiles(b, l)
    nb = t["nb"]
    rows_first = lambda a: jnp.swapaxes(a, 0, 1).reshape(a.shape[1], b // nb, nb, a.shape[2])
    seq_first = lambda a: jnp.swapaxes(a.reshape(a.shape[0], b, a.shape[3]), 0, 1)
    q, k, v, bg, zs, ga, obg, qtail, ctail = _proj_call(
        x, rows_first(qkv0), rows_first(conf0), wqkv, wrest, wbd, gpre, cqkv, decay, cdw, cvec,
        nb=nb, tl=t["tl"])
    o, s_new = _delta_call(q, k, v, bg, s0, nb=t["nbd"], c=t["c"])
    flat = lambda a: a.reshape(b * l, D_MODEL)
    y = _out_call(flat(x), flat(o), flat(zs), flat(ga), flat(obg), wout, wup, wdn, vec, tm=t["tm"])
    return y.reshape(b, l, D_MODEL), s_new, seq_first(qtail), seq_first(ctail)


def _prep_weights(w_in, conv_qkv_w, a_log, dt_bias, delta_norm_g, conf_dw_w, conf_dw_b, conf_ln_g,
                  conf_ln_b, w_out, g_pre_mix, g_post_mix, g_pre_ffn, g_post_ffn, w_up, w_down):
    qkv_w = 3 * D_MODEL
    o_bd = qkv_w
    o_z = o_bd + 2 * N_HEADS
    wqkv = w_in[:, :qkv_w].astype(BF16)
    wrest = w_in[:, o_z:].astype(BF16)
    wbd = jnp.pad(w_in[:, o_bd:o_z], ((0, 0), (0, 128 - 2 * N_HEADS))).astype(BF16)
    pad_heads = lambda t: jnp.pad(t, (N_HEADS, 128 - 2 * N_HEADS))
    decay = jnp.stack([pad_heads(a_log), pad_heads(dt_bias)]).astype(F32)
    cvec = jnp.stack([conf_dw_b, conf_ln_g, conf_ln_b])
    vec = jnp.stack([jnp.tile(delta_norm_g, N_HEADS), g_post_mix, g_pre_ffn, g_post_ffn])
    return (wqkv, wrest, wbd, g_pre_mix[None, :], conv_qkv_w, decay, conf_dw_w, cvec,
            w_out.astype(BF16), w_up.astype(BF16), w_down.astype(BF16), vec)


def kernel(x_prompt, x_sample, state_delta, state_qkv_conv, state_conf_conv, w_in, conv_qkv_w, a_log,
           dt_bias, delta_norm_g, conf_dw_w, conf_dw_b, conf_ln_g, conf_ln_b, w_out, g_pre_mix,
           g_post_mix, g_pre_ffn, g_post_ffn, w_up, w_down):
    depth = w_in.shape[0]
    bp = x_prompt.shape[0]
    xp, xs = x_prompt, x_sample
    outs = [[] for _ in range(6)]
    for l in range(depth):
        weights = _prep_weights(
            w_in[l], conv_qkv_w[l], a_log[l], dt_bias[l], delta_norm_g[l], conf_dw_w[l], conf_dw_b[l],
            conf_ln_g[l], conf_ln_b[l], w_out[l], g_pre_mix[l], g_post_mix[l], g_pre_ffn[l],
            g_post_ffn[l], w_up[l], w_down[l])
        xp, s_p, bq_p, bc_p = _layer(
            xp, jnp.zeros((bp, N_HEADS, HEAD, HEAD), state_delta.dtype),
            jnp.zeros((bp, SHORT_CONV - 1, 3 * D_MODEL), F32),
            jnp.zeros((bp, CONF_KERNEL - 1, D_MODEL), F32), weights)
        xs, s_s, bq_s, bc_s = _layer(
            xs, state_delta[l], state_qkv_conv[l], state_conf_conv[l], weights)
        for lst, val in zip(outs, (s_p, bq_p, bc_p, s_s, bq_s, bc_s)):
            lst.append(val)
    return (xp, xs) + tuple(jnp.stack(o) for o in outs)
```

```python
import functools

import jax
import jax.numpy as jnp
from jax import lax
from jax.experimental import pallas as pl
from jax.experimental.pallas import tpu as pltpu

D_MODEL = 1024
N_HEADS = 8
HEAD = 128
SHORT_CONV = 4
CONF_KERNEL = 31
D_FF = 4 * D_MODEL
EPS = 1e-6
CHUNK = 64

QKV_TAIL = 8
CONF_TAIL = 32

_C_Q, _C_K, _C_V = 0, D_MODEL, 2 * D_MODEL
_C_Z, _C_GLU_A, _C_GLU_B, _C_GATE_A, _C_GATE_B = (i * D_MODEL for i in range(5))

VMEM_LIMIT_BYTES = 56 * 1024 * 1024

F32 = jnp.float32
BF16 = jnp.bfloat16
HI = lax.Precision.HIGHEST

_NN = (((1,), (0,)), ((), ()))
_NT = (((1,), (1,)), ((), ()))
_TN = (((0,), (0,)), ((), ()))


def _mm(a, b, dims=_NN, exact=False):
    if exact:
        return lax.dot_general(a, b, dims, precision=HI, preferred_element_type=F32)
    return lax.dot_general(a.astype(BF16), b.astype(BF16), dims, preferred_element_type=F32)


def _sigmoid(x):
    return 1.0 / (1.0 + jnp.exp(-x))


def _rms(x, g):
    return x * lax.rsqrt(jnp.mean(x * x, axis=-1, keepdims=True) + EPS) * g


COL_BLOCK = 256


def _tap_groups(tail, taps):
    base = tail - (taps - 1)
    groups = [[o for o in range(base, tail + 1) if o % 8 == res] for res in range(8)]
    return base, [g for g in groups if g]


def _tap_group_sum(ext_ref, w_ref, cols, offs, base, tl):
    res = offs[0] % 8
    rows = tl + 8 if res else tl
    part = None
    for o in offs:
        term = ext_ref[:, o - res:o - res + rows, cols] * w_ref[o - base:o - base + 1, cols]
        part = term if part is None else part + term
    if res:
        nb, _, width = part.shape
        sub = lax.broadcasted_iota(jnp.int32, (nb, tl, width), 1) % 8
        merged = jnp.where(sub >= res, part[:, :tl, :], part[:, 8:, :])
        merged = merged.reshape(nb, tl // 8, 8, width)
        part = pltpu.roll(merged, 8 - res, axis=2).reshape(nb, tl, width)
    return part


def _emit_interleaved(mxu_jobs, vpu_jobs, lead):
    done = 0
    n_m, n_v = len(mxu_jobs), len(vpu_jobs)
    for i, (need, job) in enumerate(vpu_jobs):
        target = min(n_m, max(need, lead + (i * n_m) // n_v))
        while done < target:
            mxu_jobs[done]()
            done += 1
        job()
    while done < n_m:
        mxu_jobs[done]()
        done += 1


def _proj_kernel(x_ref, qkv0_ref, conf0_ref, wqkv_ref, wrest_ref, wbd_ref, gpre_ref, cqkv_ref,
                 decay_ref, cdw_ref, cvec_ref,
                 q_ref, k_ref, v_ref, bg_ref, zs_ref, ga_ref, obg_ref, qtail_ref, ctail_ref,
                 extq, extc, *, nb, tl):
    r = nb * tl
    nq = SHORT_CONV - 1
    nc = CONF_KERNEL - 1
    n_cb = D_MODEL // COL_BLOCK
    blocks = range(n_cb)
    cb_cols = lambda cb: slice(cb * COL_BLOCK, (cb + 1) * COL_BLOCK)

    @pl.when(pl.program_id(1) == 0)
    def _():
        extq[:, 0:QKV_TAIL, :] = jnp.zeros((nb, QKV_TAIL, 3 * D_MODEL), F32)
        for j in range(nq):
            extq[:, QKV_TAIL - nq + j, :] = qkv0_ref[j, 0]
        extc[:, 0:8, :] = jnp.zeros((nb, 8, D_MODEL), F32)
        for j in range(nc):
            extc[:, CONF_TAIL - nc + j, :] = conf0_ref[j, 0]

    x = x_ref[...].reshape(r, D_MODEL)
    hb = _rms(x, gpre_ref[...]).astype(BF16)
    raw = {}
    proj_jobs = {}

    def add_proj(name, w_ref, c0, width=COL_BLOCK, n_blocks=n_cb, into_extq=False):
        for cb in range(n_blocks):
            def job(cb=cb):
                lo = c0 + cb * width
                val = jnp.dot(hb, w_ref[:, lo:lo + width], preferred_element_type=F32)
                if into_extq:
                    extq[:, QKV_TAIL:QKV_TAIL + tl, lo:lo + width] = val.reshape(nb, tl, width)
                else:
                    raw[name, cb] = val
            proj_jobs[name, cb] = job

    for name, c0 in (("q", _C_Q), ("k", _C_K), ("v", _C_V)):
        add_proj(name, wqkv_ref, c0, into_extq=True)
    for name, c0 in (("glu_a", _C_GLU_A), ("glu_b", _C_GLU_B), ("z", _C_Z), ("gate_b", _C_GATE_B),
                     ("gate_a", _C_GATE_A)):
        add_proj(name, wrest_ref, c0)
    add_proj("bd", wbd_ref, 0, width=128, n_blocks=1)
    mxu_order = ([(n, cb) for cb in blocks for n in ("glu_a", "glu_b")]
                 + [(n, cb) for n in ("q", "k", "v", "z", "gate_a", "gate_b") for cb in blocks]
                 + [("bd", 0)])
    mxu_jobs = [proj_jobs[key] for key in mxu_order]
    mxu_index = {key: i + 1 for i, key in enumerate(mxu_order)}

    def vpu(needs, job):
        return (max(mxu_index[n] for n in needs) if needs else 0, job)

    qbase, qgroups = _tap_groups(QKV_TAIL, SHORT_CONV)
    qkv_jobs = []
    for name, c0, out_ref in (("q", _C_Q, q_ref), ("k", _C_K, k_ref), ("v", _C_V, v_ref)):
        for cb in blocks:
            def job(c0=c0, out_ref=out_ref, cb=cb):
                cols = slice(c0 + cb * COL_BLOCK, c0 + (cb + 1) * COL_BLOCK)
                a = None
                for offs in qgroups:
                    part = _tap_group_sum(extq, cqkv_ref, cols, offs, qbase, tl)
                    a = part if a is None else a + part
                a = a.reshape(r, COL_BLOCK)
                a = a * _sigmoid(a)
                if out_ref is not v_ref:
                    scale = HEAD ** -0.5 if out_ref is q_ref else 1.0
                    heads = []
                    for h in range(COL_BLOCK // HEAD):
                        ah = a[:, h * HEAD:(h + 1) * HEAD]
                        ah = ah * lax.rsqrt(jnp.sum(ah * ah, axis=-1, keepdims=True) + EPS)
                        heads.append(ah * scale if scale != 1.0 else ah)
                    a = jnp.concatenate(heads, axis=-1)
                out_ref[:, :, cb_cols(cb)] = a.reshape(nb, tl, COL_BLOCK)
            qkv_jobs.append(vpu([(name, cb)], job))

    glu_jobs = []
    for cb in blocks:
        def job(cb=cb):
            glu = raw["glu_a", cb] * _sigmoid(raw["glu_b", cb])
            extc[:, CONF_TAIL:CONF_TAIL + tl, cb_cols(cb)] = glu.reshape(nb, tl, COL_BLOCK)
        glu_jobs.append(vpu([("glu_a", cb), ("glu_b", cb)], job))
    cbase, cgroups = _tap_groups(CONF_TAIL, CONF_KERNEL)
    conv = [None] * n_cb
    conv_jobs = []
    for cb in blocks:
        for offs in cgroups:
            def job(cb=cb, offs=offs):
                part = _tap_group_sum(extc, cdw_ref, cb_cols(cb), offs, cbase, tl)
                conv[cb] = part if conv[cb] is None else conv[cb] + part
            conv_jobs.append(vpu([], job))

    def gate_job(name, out_ref, silu, cb):
        def job():
            t = raw[name, cb]
            t = t * _sigmoid(t) if silu else _sigmoid(t)
            out_ref[:, :, cb_cols(cb)] = t.reshape(nb, tl, COL_BLOCK)
        return vpu([(name, cb)], job)

    def beta_decay():
        bd = raw["bd", 0]
        lane = lax.broadcasted_iota(jnp.int32, bd.shape, 1)
        xg = bd + decay_ref[1:2, :]
        softplus = jnp.maximum(xg, 0.0) + jnp.log1p(jnp.exp(-jnp.abs(xg)))
        g = -jnp.exp(decay_ref[0:1, :]) * softplus
        bg_ref[...] = jnp.where(lane < N_HEADS, _sigmoid(bd), g).reshape(nb, tl, 128)

    side_jobs = [gate_job(name, out_ref, silu, cb)
                 for name, out_ref, silu in (("z", zs_ref, True), ("gate_a", ga_ref, False))
                 for cb in blocks] + [vpu([("bd", 0)], beta_decay)]

    stats = {}

    def ln_stats():
        cs = [conv[cb].reshape(r, COL_BLOCK) + cvec_ref[0:1, cb_cols(cb)] for cb in blocks]
        mu = sum(jnp.sum(c, axis=-1, keepdims=True) for c in cs) * (1.0 / D_MODEL)
        xc = [c - mu for c in cs]
        var = sum(jnp.sum(v * v, axis=-1, keepdims=True) for v in xc) * (1.0 / D_MODEL)
        stats["xc"] = xc
        stats["rstd"] = lax.rsqrt(var + EPS)
    final_jobs = [vpu([], ln_stats)]
    for cb in blocks:
        def job(cb=cb):
            cols = cb_cols(cb)
            y = stats["xc"][cb] * stats["rstd"] * cvec_ref[1:2, cols] + cvec_ref[2:3, cols]
            obg = _sigmoid(raw["gate_b", cb]) * (y * _sigmoid(y))
            obg_ref[:, :, cols] = obg.reshape(nb, tl, COL_BLOCK)
        final_jobs.append(vpu([("gate_b", cb)], job))

    vpu_jobs = [glu_jobs[0]]
    per_block = len(conv_jobs) // n_cb
    n_qkv = len(qkv_jobs)
    for i, cj in enumerate(conv_jobs):
        if i % per_block == 0 and i // per_block + 1 < n_cb:
            vpu_jobs.append(glu_jobs[i // per_block + 1])
        vpu_jobs.append(cj)
        while qkv_jobs and (n_qkv - len(qkv_jobs)) * len(conv_jobs) < n_qkv * (i + 1):
            vpu_jobs.append(qkv_jobs.pop(0))
    vpu_jobs += side_jobs + final_jobs
    _emit_interleaved(mxu_jobs, vpu_jobs, lead=2)

    for j in range(nq):
        qtail_ref[j, 0] = extq[:, QKV_TAIL + tl - nq + j, :]
    extq[:, 0:QKV_TAIL, :] = extq[:, tl:tl + QKV_TAIL, :]
    for j in range(nc):
        ctail_ref[j, 0] = extc[:, CONF_TAIL + tl - nc + j, :]
    extc[:, 0:CONF_TAIL, :] = extc[:, tl:tl + CONF_TAIL, :]


def _const_spec(shape):
    zeros = (0,) * len(shape)
    return pl.BlockSpec(shape, lambda *_: zeros, pipeline_mode=pl.Buffered(1))


def _proj_call(x, qkv0, conf0, wqkv, wrest, wbd, gpre, cqkv, decay, cdw, cvec, *, nb, tl):
    b, l, _ = x.shape
    grid = (b // nb, l // tl)
    seq = lambda width: pl.BlockSpec((nb, tl, width), lambda i, t: (i, t, 0))
    per_b = lambda rows, width: pl.BlockSpec((rows, 1, nb, width), lambda i, t: (0, i, 0, 0))
    big = jax.ShapeDtypeStruct((b, l, D_MODEL), F32)
    return pl.pallas_call(
        functools.partial(_proj_kernel, nb=nb, tl=tl),
        grid=grid,
        in_specs=[seq(D_MODEL), per_b(SHORT_CONV - 1, 3 * D_MODEL), per_b(CONF_KERNEL - 1, D_MODEL),
                  _const_spec(wqkv.shape), _const_spec(wrest.shape), _const_spec(wbd.shape),
                  _const_spec(gpre.shape), _const_spec(cqkv.shape),
                  _const_spec(decay.shape), _const_spec(cdw.shape), _const_spec(cvec.shape)],
        out_specs=[seq(D_MODEL), seq(D_MODEL), seq(D_MODEL), seq(128), seq(D_MODEL), seq(D_MODEL),
                   seq(D_MODEL), per_b(SHORT_CONV - 1, 3 * D_MODEL), per_b(CONF_KERNEL - 1, D_MODEL)],
        out_shape=[big, big, big, jax.ShapeDtypeStruct((b, l, 128), F32), big, big, big,
                   jax.ShapeDtypeStruct((SHORT_CONV - 1, b // nb, nb, 3 * D_MODEL), F32),
                   jax.ShapeDtypeStruct((CONF_KERNEL - 1, b // nb, nb, D_MODEL), F32)],
        scratch_shapes=[pltpu.VMEM((nb, QKV_TAIL + tl, 3 * D_MODEL), F32),
                        pltpu.VMEM((nb, CONF_TAIL + tl, D_MODEL), F32)],
        compiler_params=pltpu.CompilerParams(
            dimension_semantics=("arbitrary", "arbitrary"), vmem_limit_bytes=VMEM_LIMIT_BYTES),
        name="proj",
    )(x, qkv0, conf0, wqkv, wrest, wbd, gpre, cqkv, decay, cdw, cvec)


def _delta_kernel(q_ref, k_ref, v_ref, bg_ref, s0_ref, o_ref, sout_ref, s_scr, *, nb, c):
    ci = pl.program_id(1)
    n_pairs = N_HEADS // 2
    pw = 2 * HEAD

    @pl.when(ci == 0)
    def _():
        for bi in range(nb):
            for p in range(n_pairs):
                s_scr[bi, p] = jnp.concatenate([s0_ref[bi, 2 * p], s0_ref[bi, 2 * p + 1]], axis=1)

    iota = lambda shape, d: lax.broadcasted_iota(jnp.int32, shape, d)
    row = iota((c, 2 * c), 0)
    lane = iota((c, 2 * c), 1)
    col = lane % c
    first = lane < c
    causal = row >= col
    strict = row > col
    ident = (row == col).astype(F32)
    first_w = iota((c, pw), 1) < HEAD
    tri = (iota((c, c), 0) >= iota((c, c), 1)).astype(F32)
    sel = (iota((n_pairs, 128), 1) // 2 == iota((n_pairs, 128), 0) + N_HEADS // 2).astype(F32)
    odd_lane = iota((c, 128), 1) % 2 == 1

    def bdiag(x, split):
        return jnp.concatenate([jnp.where(split, x, 0.0), jnp.where(split, 0.0, x)], axis=0)

    chains = [(bi, p) for bi in range(nb) for p in range(n_pairs)]
    each = lambda f, *lists: [f(*args) for args in zip(*lists)]
    psl = lambda p: slice(p * pw, (p + 1) * pw)
    pick = lambda cols, p: jnp.where(first, cols[:, 2 * p:2 * p + 1], cols[:, 2 * p + 1:2 * p + 2])
    pick_w = lambda cols, p: jnp.where(first_w[:cols.shape[0]], cols[:, 2 * p:2 * p + 1],
                                       cols[:, 2 * p + 1:2 * p + 2])

    bgs = [bg_ref[bi] for bi in range(nb)]
    gams = [_mm(tri, bg, exact=True) for bg in bgs]
    gam_ts = [_mm(sel, jnp.concatenate([jnp.where(odd_lane, 0.0, g), jnp.where(odd_lane, g, 0.0)], axis=0),
                  _NT, exact=True) for g in gams]
    gcols = [g[:, N_HEADS:2 * N_HEADS] for g in gams]
    q = [q_ref[bi, :, psl(p)] for bi, p in chains]
    k = [k_ref[bi, :, psl(p)] for bi, p in chains]
    v = [v_ref[bi, :, psl(p)] for bi, p in chains]
    beta = [pick(bgs[bi], p) for bi, p in chains]
    beta_w = [pick_w(bgs[bi], p) for bi, p in chains]
    gc = [pick(gcols[bi], p) for bi, p in chains]
    gc_w = [pick_w(gcols[bi], p) for bi, p in chains]
    gl_w = [pick_w(gcols[bi][c - 1:c, :], p) for bi, p in chains]
    gr = [gam_ts[bi][p:p + 1, :] for bi, p in chains]
    decay = each(lambda a, b: jnp.where(causal, jnp.exp(jnp.where(causal, a - b, 0.0)), 0.0), gc, gr)
    kb = each(lambda a: a.astype(BF16), k)
    kkqk = each(lambda a, b: _mm(jnp.concatenate([a, b.astype(BF16)], axis=0), bdiag(a, first_w), _NT),
                kb, q)
    n = each(lambda b, a, d: jnp.where(strict, -(b * a[:c] * d), 0.0), beta, kkqk, decay)
    eg_w = each(jnp.exp, gc_w)
    t = each(lambda a: ident + a, n)
    pw_ = each(lambda a: _mm(a, bdiag(a, first)), n)
    span = 4
    while span <= c:
        pbd = each(lambda a: bdiag(a, first), pw_)
        if span < c:
            both = each(lambda a, b, m: _mm(jnp.concatenate([a, b], axis=0), m), t, pw_, pbd)
            t = each(lambda a, r: a + r[:c], t, both)
            pw_ = each(lambda r: r[c:], both)
        else:
            t = each(lambda a, m: a + _mm(a, m), t, pbd)
        span *= 2
    rhs = each(lambda b, vv, e, kk_: jnp.concatenate(
        [jnp.concatenate([b[:, :HEAD] * vv[:, :HEAD], (b * e * kk_)[:, :HEAD]], axis=1),
         jnp.concatenate([b[:, HEAD:] * vv[:, HEAD:], (b * e * kk_)[:, HEAD:]], axis=1)], axis=0),
        beta_w, v, eg_w, k)
    sol = each(lambda a, r: _mm(bdiag(a, first), r), t, rhs)
    s = [s_scr[bi, p] for bi, p in chains]
    first_s = iota((HEAD, pw), 1) < HEAD
    sbd = each(lambda a: bdiag(a, first_s).astype(BF16), s)
    kd = each(lambda x: jnp.concatenate([x[:c, HEAD:], x[c:, HEAD:]], axis=1), sol)
    vw = each(lambda x: jnp.concatenate([x[:c, :HEAD], x[c:, :HEAD]], axis=1), sol)
    ks_qs = each(lambda a, qq, e, m: _mm(jnp.concatenate([a, qq * e], axis=0), m), kd, q, eg_w, sbd)
    u = each(lambda a, r: a - r[:c], vw, ks_qs)
    ubd = each(lambda a: bdiag(a, first_w).astype(BF16), u)
    o = each(lambda r, a, d, m: r[c:] + _mm(a[c:] * d, m), ks_qs, kkqk, decay, ubd)
    ko = each(lambda kk_, gl, g: kk_ * jnp.exp(gl - g), k, gl_w, gc_w)
    s_new = each(lambda gl, a, kk_, m: jnp.exp(gl) * a + _mm(
        jnp.concatenate([kk_[:, :HEAD], kk_[:, HEAD:]], axis=0), m, _TN), gl_w, s, ko, ubd)
    for (bi, p), o_p, s_p in zip(chains, o, s_new):
        o_ref[bi, :, psl(p)] = o_p
        s_scr[bi, p] = s_p

    @pl.when(ci == pl.num_programs(1) - 1)
    def _():
        for bi in range(nb):
            for p in range(n_pairs):
                sout_ref[bi, 2 * p] = s_scr[bi, p, :, 0:HEAD]
                sout_ref[bi, 2 * p + 1] = s_scr[bi, p, :, HEAD:pw]


def _delta_call(q, k, v, bg, s0, *, nb, c):
    b, l, _ = q.shape
    seq = lambda width: pl.BlockSpec((nb, c, width), lambda i, t: (i, t, 0))
    st = pl.BlockSpec((nb, N_HEADS, HEAD, HEAD), lambda i, t: (i, 0, 0, 0))
    return pl.pallas_call(
        functools.partial(_delta_kernel, nb=nb, c=c),
        grid=(b // nb, l // c),
        in_specs=[seq(D_MODEL), seq(D_MODEL), seq(D_MODEL), seq(128), st],
        out_specs=[seq(D_MODEL), st],
        out_shape=[jax.ShapeDtypeStruct((b, l, D_MODEL), F32),
                   jax.ShapeDtypeStruct((b, N_HEADS, HEAD, HEAD), F32)],
        scratch_shapes=[pltpu.VMEM((nb, N_HEADS // 2, HEAD, 2 * HEAD), F32)],
        compiler_params=pltpu.CompilerParams(
            dimension_semantics=("arbitrary", "arbitrary"), vmem_limit_bytes=VMEM_LIMIT_BYTES),
        name="delta",
    )(q, k, v, bg, s0)


FF_SPLIT = 4


def _out_kernel(x_ref, o_ref, zs_ref, ga_ref, obg_ref, wout_ref, wup_ref, wdn_ref, vec_ref, y_ref):
    o = o_ref[...]
    parts = []
    for h in range(N_HEADS):
        oh = o[:, h * HEAD:(h + 1) * HEAD]
        parts.append(oh * lax.rsqrt(jnp.mean(oh * oh, axis=-1, keepdims=True) + EPS))
    oa = jnp.concatenate(parts, axis=-1) * vec_ref[0:1, :] * zs_ref[...]
    mixed = ga_ref[...] * oa + obg_ref[...]
    a = jnp.dot(mixed.astype(BF16), wout_ref[...], preferred_element_type=F32)
    x1 = x_ref[...] + _rms(a, vec_ref[1:2, :])
    hn = _rms(x1, vec_ref[2:3, :]).astype(BF16)
    fw = D_FF // FF_SPLIT
    acc = None
    for i in range(FF_SPLIT):
        hf = jnp.dot(hn, wup_ref[:, i * fw:(i + 1) * fw], preferred_element_type=F32)
        f = jnp.square(jnp.maximum(hf, 0.0)).astype(BF16)
        d = jnp.dot(f, wdn_ref[i * fw:(i + 1) * fw, :], preferred_element_type=F32)
        acc = d if acc is None else acc + d
    y_ref[...] = x1 + _rms(acc, vec_ref[3:4, :])


def _out_call(x, o, zs, ga, obg, wout, wup, wdn, vec, *, tm):
    n = x.shape[0]
    tok = pl.BlockSpec((tm, D_MODEL), lambda i: (i, 0))
    return pl.pallas_call(
        _out_kernel,
        grid=(n // tm,),
        in_specs=[tok, tok, tok, tok, tok, _const_spec(wout.shape), _const_spec(wup.shape),
                  _const_spec(wdn.shape), _const_spec(vec.shape)],
        out_specs=tok,
        out_shape=jax.ShapeDtypeStruct((n, D_MODEL), F32),
        compiler_params=pltpu.CompilerParams(
            dimension_semantics=("arbitrary",), vmem_limit_bytes=VMEM_LIMIT_BYTES),
        name="out",
    )(x, o, zs, ga, obg, wout, wup, wdn, vec)


def _tiles(b, l):
    if l >= 256:
        proj = (1, 256)
    else:
        proj = (min(b, 128 // l), l)
    delta = (min(b, 8), min(l, CHUNK))
    tm = min(b * l, 512)
    return dict(nb=proj[0], tl=proj[1], nbd=delta[0], c=delta[1], tm=tm)


def _layer(x, s0, qkv0, conf0, weights):
    wqkv, wrest, wbd, gpre, cqkv, decay, cdw, cvec, wout, wup, wdn, vec = weights
    b, l, _ = x.shape
    t = _tiles(b, l)
    nb = t["nb"]
    rows_first = lambda a: jnp.swapaxes(a, 0, 1).reshape(a.shape[1], b // nb, nb, a.shape[2])
    seq_first = lambda a: jnp.swapaxes(a.reshape(a.shape[0], b, a.shape[3]), 0, 1)
    q, k, v, bg, zs, ga, obg, qtail, ctail = _proj_call(
        x, rows_first(qkv0), rows_first(conf0), wqkv, wrest, wbd, gpre, cqkv, decay, cdw, cvec,
        nb=nb, tl=t["tl"])
    o, s_new = _delta_call(q, k, v, bg, s0, nb=t["nbd"], c=t["c"])
    flat = lambda a: a.reshape(b * l, D_MODEL)
    y = _out_call(flat(x), flat(o), flat(zs), flat(ga), flat(obg), wout, wup, wdn, vec, tm=t["tm"])
    return y.reshape(b, l, D_MODEL), s_new, seq_first(qtail), seq_first(ctail)


_QKV_W = 3 * D_MODEL
_BD_W = 2 * N_HEADS
W_IN_ROWS = 128


def _split_w_in_kernel(w_ref, wqkv_ref, wrest_ref, wbd_ref):
    wqkv_ref[...] = w_ref[:, 0:_QKV_W].astype(BF16)
    wrest_ref[...] = w_ref[:, _QKV_W + _BD_W:].astype(BF16)
    bd = w_ref[:, _QKV_W:_QKV_W + _BD_W]
    wbd_ref[...] = jnp.concatenate(
        [bd, jnp.zeros((bd.shape[0], 128 - _BD_W), F32)], axis=1).astype(BF16)


def _split_w_in(w_in):
    rows, width = w_in.shape
    rest_w = width - _QKV_W - _BD_W
    blk = lambda w: pl.BlockSpec((W_IN_ROWS, w), lambda i: (i, 0))
    return pl.pallas_call(
        _split_w_in_kernel,
        grid=(rows // W_IN_ROWS,),
        in_specs=[blk(width)],
        out_specs=[blk(_QKV_W), blk(rest_w), blk(128)],
        out_shape=[jax.ShapeDtypeStruct((rows, _QKV_W), BF16),
                   jax.ShapeDtypeStruct((rows, rest_w), BF16),
                   jax.ShapeDtypeStruct((rows, 128), BF16)],
        compiler_params=pltpu.CompilerParams(
            dimension_semantics=("arbitrary",), vmem_limit_bytes=VMEM_LIMIT_BYTES),
        name="split_w_in",
    )(w_in)


def _prep_weights(w_in, conv_qkv_w, a_log, dt_bias, delta_norm_g, conf_dw_w, conf_dw_b, conf_ln_g,
                  conf_ln_b, w_out, g_pre_mix, g_post_mix, g_pre_ffn, g_post_ffn, w_up, w_down):
    wqkv, wrest, wbd = _split_w_in(w_in)
    pad_heads = lambda t: jnp.pad(t, (N_HEADS, 128 - 2 * N_HEADS))
    decay = jnp.stack([pad_heads(a_log), pad_heads(dt_bias)]).astype(F32)
    cvec = jnp.stack([conf_dw_b, conf_ln_g, conf_ln_b])
    vec = jnp.stack([jnp.tile(delta_norm_g, N_HEADS), g_post_mix, g_pre_ffn, g_post_ffn])
    return (wqkv, wrest, wbd, g_pre_mix[None, :], conv_qkv_w, decay, conf_dw_w, cvec,
            w_out.astype(BF16), w_up.astype(BF16), w_down.astype(BF16), vec)


def kernel(x_prompt, x_sample, state_delta, state_qkv_conv, state_conf_conv, w_in, conv_qkv_w, a_log,
           dt_bias, delta_norm_g, conf_dw_w, conf_dw_b, conf_ln_g, conf_ln_b, w_out, g_pre_mix,
           g_post_mix, g_pre_ffn, g_post_ffn, w_up, w_down):
    depth = w_in.shape[0]
    bp = x_prompt.shape[0]
    xp, xs = x_prompt, x_sample
    outs = [[] for _ in range(6)]
    for l in range(depth):
        weights = _prep_weights(
            w_in[l], conv_qkv_w[l], a_log[l], dt_bias[l], delta_norm_g[l], conf_dw_w[l], conf_dw_b[l],
            conf_ln_g[l], conf_ln_b[l], w_out[l], g_pre_mix[l], g_post_mix[l], g_pre_ffn[l],
            g_post_ffn[l], w_up[l], w_down[l])
        xp, s_p, bq_p, bc_p = _layer(
            xp, jnp.zeros((bp, N_HEADS, HEAD, HEAD), state_delta.dtype),
            jnp.zeros((bp, SHORT_CONV - 1, 3 * D_MODEL), F32),
            jnp.zeros((bp, CONF_KERNEL - 1, D_MODEL), F32), weights)
        xs, s_s, bq_s, bc_s = _layer(
            xs, state_delta[l], state_qkv_conv[l], state_conf_conv[l], weights)
        for lst, val in zip(outs, (s_p, bq_p, bc_p, s_s, bq_s, bc_s)):
            lst.append(val)
    return (xp, xs) + tuple(jnp.stack(o) for o in outs)
```

```python
import functools

import jax
import jax.numpy as jnp
from jax import lax
from jax.experimental import pallas as pl
from jax.experimental.pallas import tpu as pltpu

D_MODEL = 1024
N_HEADS = 8
HEAD = 128
SHORT_CONV = 4
CONF_KERNEL = 31
D_FF = 4 * D_MODEL
EPS = 1e-6
CHUNK = 64

QKV_TAIL = 8
CONF_TAIL = 32

_C_Q, _C_K, _C_V = 0, D_MODEL, 2 * D_MODEL
_C_Z, _C_GLU_A, _C_GLU_B, _C_GATE_A, _C_GATE_B = (i * D_MODEL for i in range(5))

VMEM_LIMIT_BYTES = 56 * 1024 * 1024

F32 = jnp.float32
BF16 = jnp.bfloat16
HI = lax.Precision.HIGHEST

_NN = (((1,), (0,)), ((), ()))
_NT = (((1,), (1,)), ((), ()))
_TN = (((0,), (0,)), ((), ()))


def _mm(a, b, dims=_NN, exact=False):
    if exact:
        return lax.dot_general(a, b, dims, precision=HI, preferred_element_type=F32)
    return lax.dot_general(a.astype(BF16), b.astype(BF16), dims, preferred_element_type=F32)


def _sigmoid(x):
    return 1.0 / (1.0 + jnp.exp(-x))


def _rms(x, g):
    return x * lax.rsqrt(jnp.mean(x * x, axis=-1, keepdims=True) + EPS) * g


COL_BLOCK = 256


def _tap_groups(tail, taps):
    base = tail - (taps - 1)
    groups = [[o for o in range(base, tail + 1) if o % 8 == res] for res in range(8)]
    return base, [g for g in groups if g]


def _tap_group_sum(ext_ref, w_ref, cols, offs, base, tl):
    res = offs[0] % 8
    rows = tl + 8 if res else tl
    part = None
    for o in offs:
        term = ext_ref[:, o - res:o - res + rows, cols] * w_ref[o - base:o - base + 1, cols]
        part = term if part is None else part + term
    if res:
        nb, _, width = part.shape
        sub = lax.broadcasted_iota(jnp.int32, (nb, tl, width), 1) % 8
        merged = jnp.where(sub >= res, part[:, :tl, :], part[:, 8:, :])
        merged = merged.reshape(nb, tl // 8, 8, width)
        part = pltpu.roll(merged, 8 - res, axis=2).reshape(nb, tl, width)
    return part


def _emit_interleaved(mxu_jobs, vpu_jobs, lead):
    done = 0
    n_m, n_v = len(mxu_jobs), len(vpu_jobs)
    for i, (need, job) in enumerate(vpu_jobs):
        target = min(n_m, max(need, lead + (i * n_m) // n_v))
        while done < target:
            mxu_jobs[done]()
            done += 1
        job()
    while done < n_m:
        mxu_jobs[done]()
        done += 1


def _proj_kernel(x_ref, qkv0_ref, conf0_ref, wqkv_ref, wrest_ref, wbd_ref, gpre_ref, cqkv_ref,
                 decay_ref, cdw_ref, cvec_ref,
                 q_ref, k_ref, v_ref, bg_ref, zs_ref, ga_ref, obg_ref, qtail_ref, ctail_ref,
                 extq, extc, *, nb, tl):
    r = nb * tl
    nq = SHORT_CONV - 1
    nc = CONF_KERNEL - 1
    n_cb = D_MODEL // COL_BLOCK
    blocks = range(n_cb)
    cb_cols = lambda cb: slice(cb * COL_BLOCK, (cb + 1) * COL_BLOCK)

    @pl.when(pl.program_id(1) == 0)
    def _():
        extq[:, 0:QKV_TAIL, :] = jnp.zeros((nb, QKV_TAIL, 3 * D_MODEL), F32)
        for j in range(nq):
            extq[:, QKV_TAIL - nq + j, :] = qkv0_ref[j, 0]
        extc[:, 0:8, :] = jnp.zeros((nb, 8, D_MODEL), F32)
        for j in range(nc):
            extc[:, CONF_TAIL - nc + j, :] = conf0_ref[j, 0]

    x = x_ref[...].reshape(r, D_MODEL)
    hb = _rms(x, gpre_ref[...]).astype(BF16)
    raw = {}
    proj_jobs = {}

    def add_proj(name, w_ref, c0, width=COL_BLOCK, n_blocks=n_cb, into_extq=False):
        for cb in range(n_blocks):
            def job(cb=cb):
                lo = c0 + cb * width
                val = jnp.dot(hb, w_ref[:, lo:lo + width], preferred_element_type=F32)
                if into_extq:
                    extq[:, QKV_TAIL:QKV_TAIL + tl, lo:lo + width] = val.reshape(nb, tl, width)
                else:
                    raw[name, cb] = val
            proj_jobs[name, cb] = job

    for name, c0 in (("q", _C_Q), ("k", _C_K), ("v", _C_V)):
        add_proj(name, wqkv_ref, c0, into_extq=True)
    for name, c0 in (("glu_a", _C_GLU_A), ("glu_b", _C_GLU_B), ("z", _C_Z), ("gate_b", _C_GATE_B),
                     ("gate_a", _C_GATE_A)):
        add_proj(name, wrest_ref, c0)
    add_proj("bd", wbd_ref, 0, width=128, n_blocks=1)
    mxu_order = ([(n, cb) for cb in blocks for n in ("glu_a", "glu_b")]
                 + [(n, cb) for n in ("q", "k", "v", "z", "gate_a", "gate_b") for cb in blocks]
                 + [("bd", 0)])
    mxu_jobs = [proj_jobs[key] for key in mxu_order]
    mxu_index = {key: i + 1 for i, key in enumerate(mxu_order)}

    def vpu(needs, job):
        return (max(mxu_index[n] for n in needs) if needs else 0, job)

    qbase, qgroups = _tap_groups(QKV_TAIL, SHORT_CONV)
    qkv_jobs = []
    for name, c0, out_ref in (("q", _C_Q, q_ref), ("k", _C_K, k_ref), ("v", _C_V, v_ref)):
        for cb in blocks:
            def job(c0=c0, out_ref=out_ref, cb=cb):
                cols = slice(c0 + cb * COL_BLOCK, c0 + (cb + 1) * COL_BLOCK)
                a = None
                for offs in qgroups:
                    part = _tap_group_sum(extq, cqkv_ref, cols, offs, qbase, tl)
                    a = part if a is None else a + part
                a = a.reshape(r, COL_BLOCK)
                a = a * _sigmoid(a)
                if out_ref is not v_ref:
                    scale = HEAD ** -0.5 if out_ref is q_ref else 1.0
                    heads = []
                    for h in range(COL_BLOCK // HEAD):
                        ah = a[:, h * HEAD:(h + 1) * HEAD]
                        ah = ah * lax.rsqrt(jnp.sum(ah * ah, axis=-1, keepdims=True) + EPS)
                        heads.append(ah * scale if scale != 1.0 else ah)
                    a = jnp.concatenate(heads, axis=-1)
                out_ref[:, :, cb_cols(cb)] = a.reshape(nb, tl, COL_BLOCK)
            qkv_jobs.append(vpu([(name, cb)], job))

    glu_jobs = []
    for cb in blocks:
        def job(cb=cb):
            glu = raw["glu_a", cb] * _sigmoid(raw["glu_b", cb])
            extc[:, CONF_TAIL:CONF_TAIL + tl, cb_cols(cb)] = glu.reshape(nb, tl, COL_BLOCK)
        glu_jobs.append(vpu([("glu_a", cb), ("glu_b", cb)], job))
    cbase, cgroups = _tap_groups(CONF_TAIL, CONF_KERNEL)
    conv = [None] * n_cb
    conv_jobs = []
    for cb in blocks:
        for offs in cgroups:
            def job(cb=cb, offs=offs):
                part = _tap_group_sum(extc, cdw_ref, cb_cols(cb), offs, cbase, tl)
                conv[cb] = part if conv[cb] is None else conv[cb] + part
            conv_jobs.append(vpu([], job))

    def gate_job(name, out_ref, silu, cb):
        def job():
            t = raw[name, cb]
            t = t * _sigmoid(t) if silu else _sigmoid(t)
            out_ref[:, :, cb_cols(cb)] = t.reshape(nb, tl, COL_BLOCK)
        return vpu([(name, cb)], job)

    def beta_decay():
        lane = lax.broadcasted_iota(jnp.int32, (r, 128), 1)
        bd = jnp.where(lane < 2 * N_HEADS, raw["bd", 0], 0.0)
        xg = bd + decay_ref[1:2, :]
        softplus = jnp.maximum(xg, 0.0) + jnp.log1p(jnp.exp(-jnp.abs(xg)))
        g = -jnp.exp(decay_ref[0:1, :]) * softplus
        bg_ref[...] = jnp.where(lane < N_HEADS, _sigmoid(bd), g).reshape(nb, tl, 128)

    side_jobs = [gate_job(name, out_ref, silu, cb)
                 for name, out_ref, silu in (("z", zs_ref, True), ("gate_a", ga_ref, False))
                 for cb in blocks] + [vpu([("bd", 0)], beta_decay)]

    stats = {}

    def ln_stats():
        cs = [conv[cb].reshape(r, COL_BLOCK) + cvec_ref[0:1, cb_cols(cb)] for cb in blocks]
        mu = sum(jnp.sum(c, axis=-1, keepdims=True) for c in cs) * (1.0 / D_MODEL)
        xc = [c - mu for c in cs]
        var = sum(jnp.sum(v * v, axis=-1, keepdims=True) for v in xc) * (1.0 / D_MODEL)
        stats["xc"] = xc
        stats["rstd"] = lax.rsqrt(var + EPS)
    final_jobs = [vpu([], ln_stats)]
    for cb in blocks:
        def job(cb=cb):
            cols = cb_cols(cb)
            y = stats["xc"][cb] * stats["rstd"] * cvec_ref[1:2, cols] + cvec_ref[2:3, cols]
            obg = _sigmoid(raw["gate_b", cb]) * (y * _sigmoid(y))
            obg_ref[:, :, cols] = obg.reshape(nb, tl, COL_BLOCK)
        final_jobs.append(vpu([("gate_b", cb)], job))

    vpu_jobs = [glu_jobs[0]]
    per_block = len(conv_jobs) // n_cb
    n_qkv = len(qkv_jobs)
    for i, cj in enumerate(conv_jobs):
        if i % per_block == 0 and i // per_block + 1 < n_cb:
            vpu_jobs.append(glu_jobs[i // per_block + 1])
        vpu_jobs.append(cj)
        while qkv_jobs and (n_qkv - len(qkv_jobs)) * len(conv_jobs) < n_qkv * (i + 1):
            vpu_jobs.append(qkv_jobs.pop(0))
    vpu_jobs += side_jobs + final_jobs
    _emit_interleaved(mxu_jobs, vpu_jobs, lead=2)

    for j in range(nq):
        qtail_ref[j, 0] = extq[:, QKV_TAIL + tl - nq + j, :]
    extq[:, 0:QKV_TAIL, :] = extq[:, tl:tl + QKV_TAIL, :]
    for j in range(nc):
        ctail_ref[j, 0] = extc[:, CONF_TAIL + tl - nc + j, :]
    extc[:, 0:CONF_TAIL, :] = extc[:, tl:tl + CONF_TAIL, :]


def _const_spec(shape):
    zeros = (0,) * len(shape)
    return pl.BlockSpec(shape, lambda *_: zeros, pipeline_mode=pl.Buffered(1))


def _proj_call(x, qkv0, conf0, wqkv, wrest, wbd, gpre, cqkv, decay, cdw, cvec, *, nb, tl):
    b, l, _ = x.shape
    grid = (b // nb, l // tl)
    seq = lambda width: pl.BlockSpec((nb, tl, width), lambda i, t: (i, t, 0))
    per_b = lambda rows, width: pl.BlockSpec((rows, 1, nb, width), lambda i, t: (0, i, 0, 0))
    big = jax.ShapeDtypeStruct((b, l, D_MODEL), F32)
    return pl.pallas_call(
        functools.partial(_proj_kernel, nb=nb, tl=tl),
        grid=grid,
        in_specs=[seq(D_MODEL), per_b(SHORT_CONV - 1, 3 * D_MODEL), per_b(CONF_KERNEL - 1, D_MODEL),
                  _const_spec(wqkv.shape), _const_spec(wrest.shape), _const_spec(wbd.shape),
                  _const_spec(gpre.shape), _const_spec(cqkv.shape),
                  _const_spec(decay.shape), _const_spec(cdw.shape), _const_spec(cvec.shape)],
        out_specs=[seq(D_MODEL), seq(D_MODEL), seq(D_MODEL), seq(128), seq(D_MODEL), seq(D_MODEL),
                   seq(D_MODEL), per_b(SHORT_CONV - 1, 3 * D_MODEL), per_b(CONF_KERNEL - 1, D_MODEL)],
        out_shape=[big, big, big, jax.ShapeDtypeStruct((b, l, 128), F32), big, big, big,
                   jax.ShapeDtypeStruct((SHORT_CONV - 1, b // nb, nb, 3 * D_MODEL), F32),
                   jax.ShapeDtypeStruct((CONF_KERNEL - 1, b // nb, nb, D_MODEL), F32)],
        scratch_shapes=[pltpu.VMEM((nb, QKV_TAIL + tl, 3 * D_MODEL), F32),
                        pltpu.VMEM((nb, CONF_TAIL + tl, D_MODEL), F32)],
        compiler_params=pltpu.CompilerParams(
            dimension_semantics=("arbitrary", "arbitrary"), vmem_limit_bytes=VMEM_LIMIT_BYTES),
        name="proj",
    )(x, qkv0, conf0, wqkv, wrest, wbd, gpre, cqkv, decay, cdw, cvec)


def _delta_kernel(q_ref, k_ref, v_ref, bg_ref, s0_ref, o_ref, sout_ref, s_scr, *, nb, c):
    ci = pl.program_id(1)
    n_pairs = N_HEADS // 2
    pw = 2 * HEAD

    @pl.when(ci == 0)
    def _():
        for bi in range(nb):
            for p in range(n_pairs):
                s_scr[bi, p] = jnp.concatenate([s0_ref[bi, 2 * p], s0_ref[bi, 2 * p + 1]], axis=1)

    iota = lambda shape, d: lax.broadcasted_iota(jnp.int32, shape, d)
    row = iota((c, 2 * c), 0)
    lane = iota((c, 2 * c), 1)
    col = lane % c
    first = lane < c
    causal = row >= col
    strict = row > col
    ident = (row == col).astype(F32)
    first_w = iota((c, pw), 1) < HEAD
    tri = (iota((c, c), 0) >= iota((c, c), 1)).astype(F32)
    sel = (iota((n_pairs, 128), 1) // 2 == iota((n_pairs, 128), 0) + N_HEADS // 2).astype(F32)
    odd_lane = iota((c, 128), 1) % 2 == 1

    def bdiag(x, split):
        return jnp.concatenate([jnp.where(split, x, 0.0), jnp.where(split, 0.0, x)], axis=0)

    chains = [(bi, p) for bi in range(nb) for p in range(n_pairs)]
    each = lambda f, *lists: [f(*args) for args in zip(*lists)]
    psl = lambda p: slice(p * pw, (p + 1) * pw)
    pick = lambda cols, p: jnp.where(first, cols[:, 2 * p:2 * p + 1], cols[:, 2 * p + 1:2 * p + 2])
    pick_w = lambda cols, p: jnp.where(first_w[:cols.shape[0]], cols[:, 2 * p:2 * p + 1],
                                       cols[:, 2 * p + 1:2 * p + 2])

    bgs = [bg_ref[bi] for bi in range(nb)]
    gams = [_mm(tri, bg, exact=True) for bg in bgs]
    gam_ts = [_mm(sel, jnp.concatenate([jnp.where(odd_lane, 0.0, g), jnp.where(odd_lane, g, 0.0)], axis=0),
                  _NT, exact=True) for g in gams]
    gcols = [g[:, N_HEADS:2 * N_HEADS] for g in gams]
    q = [q_ref[bi, :, psl(p)] for bi, p in chains]
    k = [k_ref[bi, :, psl(p)] for bi, p in chains]
    v = [v_ref[bi, :, psl(p)] for bi, p in chains]
    beta = [pick(bgs[bi], p) for bi, p in chains]
    beta_w = [pick_w(bgs[bi], p) for bi, p in chains]
    gc = [pick(gcols[bi], p) for bi, p in chains]
    gc_w = [pick_w(gcols[bi], p) for bi, p in chains]
    gl_w = [pick_w(gcols[bi][c - 1:c, :], p) for bi, p in chains]
    gr = [gam_ts[bi][p:p + 1, :] for bi, p in chains]
    decay = each(lambda a, b: jnp.where(causal, jnp.exp(jnp.where(causal, a - b, 0.0)), 0.0), gc, gr)
    kb = each(lambda a: a.astype(BF16), k)
    kkqk = each(lambda a, b: _mm(jnp.concatenate([a, b.astype(BF16)], axis=0), bdiag(a, first_w), _NT),
                kb, q)
    n = each(lambda b, a, d: jnp.where(strict, -(b * a[:c] * d), 0.0), beta, kkqk, decay)
    eg_w = each(jnp.exp, gc_w)
    t = each(lambda a: ident + a, n)
    pw_ = each(lambda a: _mm(a, bdiag(a, first)), n)
    span = 4
    while span <= c:
        pbd = each(lambda a: bdiag(a, first), pw_)
        if span < c:
            both = each(lambda a, b, m: _mm(jnp.concatenate([a, b], axis=0), m), t, pw_, pbd)
            t = each(lambda a, r: a + r[:c], t, both)
            pw_ = each(lambda r: r[c:], both)
        else:
            t = each(lambda a, m: a + _mm(a, m), t, pbd)
        span *= 2
    rhs = each(lambda b, vv, e, kk_: jnp.concatenate(
        [jnp.concatenate([b[:, :HEAD] * vv[:, :HEAD], (b * e * kk_)[:, :HEAD]], axis=1),
         jnp.concatenate([b[:, HEAD:] * vv[:, HEAD:], (b * e * kk_)[:, HEAD:]], axis=1)], axis=0),
        beta_w, v, eg_w, k)
    sol = each(lambda a, r: _mm(bdiag(a, first), r), t, rhs)
    s = [s_scr[bi, p] for bi, p in chains]
    first_s = iota((HEAD, pw), 1) < HEAD
    sbd = each(lambda a: bdiag(a, first_s).astype(BF16), s)
    kd = each(lambda x: jnp.concatenate([x[:c, HEAD:], x[c:, HEAD:]], axis=1), sol)
    vw = each(lambda x: jnp.concatenate([x[:c, :HEAD], x[c:, :HEAD]], axis=1), sol)
    ks_qs = each(lambda a, qq, e, m: _mm(jnp.concatenate([a, qq * e], axis=0), m), kd, q, eg_w, sbd)
    u = each(lambda a, r: a - r[:c], vw, ks_qs)
    ubd = each(lambda a: bdiag(a, first_w).astype(BF16), u)
    o = each(lambda r, a, d, m: r[c:] + _mm(a[c:] * d, m), ks_qs, kkqk, decay, ubd)
    ko = each(lambda kk_, gl, g: kk_ * jnp.exp(gl - g), k, gl_w, gc_w)
    s_new = each(lambda gl, a, kk_, m: jnp.exp(gl) * a + _mm(
        jnp.concatenate([kk_[:, :HEAD], kk_[:, HEAD:]], axis=0), m, _TN), gl_w, s, ko, ubd)
    for (bi, p), o_p, s_p in zip(chains, o, s_new):
        o_ref[bi, :, psl(p)] = o_p
        s_scr[bi, p] = s_p

    @pl.when(ci == pl.num_programs(1) - 1)
    def _():
        for bi in range(nb):
            for p in range(n_pairs):
                sout_ref[bi, 2 * p] = s_scr[bi, p, :, 0:HEAD]
                sout_ref[bi, 2 * p + 1] = s_scr[bi, p, :, HEAD:pw]


def _delta_call(q, k, v, bg, s0, *, nb, c):
    b, l, _ = q.shape
    seq = lambda width: pl.BlockSpec((nb, c, width), lambda i, t: (i, t, 0))
    st = pl.BlockSpec((nb, N_HEADS, HEAD, HEAD), lambda i, t: (i, 0, 0, 0))
    return pl.pallas_call(
        functools.partial(_delta_kernel, nb=nb, c=c),
        grid=(b // nb, l // c),
        in_specs=[seq(D_MODEL), seq(D_MODEL), seq(D_MODEL), seq(128), st],
        out_specs=[seq(D_MODEL), st],
        out_shape=[jax.ShapeDtypeStruct((b, l, D_MODEL), F32),
                   jax.ShapeDtypeStruct((b, N_HEADS, HEAD, HEAD), F32)],
        scratch_shapes=[pltpu.VMEM((nb, N_HEADS // 2, HEAD, 2 * HEAD), F32)],
        compiler_params=pltpu.CompilerParams(
            dimension_semantics=("arbitrary", "arbitrary"), vmem_limit_bytes=VMEM_LIMIT_BYTES),
        name="delta",
    )(q, k, v, bg, s0)


FF_SPLIT = 4


def _out_kernel(x_ref, o_ref, zs_ref, ga_ref, obg_ref, wout_ref, wup_ref, wdn_ref, vec_ref, y_ref):
    o = o_ref[...]
    parts = []
    for h in range(N_HEADS):
        oh = o[:, h * HEAD:(h + 1) * HEAD]
        parts.append(oh * lax.rsqrt(jnp.mean(oh * oh, axis=-1, keepdims=True) + EPS))
    oa = jnp.concatenate(parts, axis=-1) * vec_ref[0:1, :] * zs_ref[...]
    mixed = ga_ref[...] * oa + obg_ref[...]
    a = jnp.dot(mixed.astype(BF16), wout_ref[...], preferred_element_type=F32)
    x1 = x_ref[...] + _rms(a, vec_ref[1:2, :])
    hn = _rms(x1, vec_ref[2:3, :]).astype(BF16)
    fw = D_FF // FF_SPLIT
    acc = None
    for i in range(FF_SPLIT):
        hf = jnp.dot(hn, wup_ref[:, i * fw:(i + 1) * fw], preferred_element_type=F32)
        f = jnp.square(jnp.maximum(hf, 0.0)).astype(BF16)
        d = jnp.dot(f, wdn_ref[i * fw:(i + 1) * fw, :], preferred_element_type=F32)
        acc = d if acc is None else acc + d
    y_ref[...] = x1 + _rms(acc, vec_ref[3:4, :])


def _out_call(x, o, zs, ga, obg, wout, wup, wdn, vec, *, tm):
    n = x.shape[0]
    tok = pl.BlockSpec((tm, D_MODEL), lambda i: (i, 0))
    return pl.pallas_call(
        _out_kernel,
        grid=(n // tm,),
        in_specs=[tok, tok, tok, tok, tok, _const_spec(wout.shape), _const_spec(wup.shape),
                  _const_spec(wdn.shape), _const_spec(vec.shape)],
        out_specs=tok,
        out_shape=jax.ShapeDtypeStruct((n, D_MODEL), F32),
        compiler_params=pltpu.CompilerParams(
            dimension_semantics=("arbitrary",), vmem_limit_bytes=VMEM_LIMIT_BYTES),
        name="out",
    )(x, o, zs, ga, obg, wout, wup, wdn, vec)


def _tiles(b, l):
    if l >= 256:
        proj = (1, 256)
    else:
        proj = (min(b, 128 // l), l)
    delta = (min(b, 8), min(l, CHUNK))
    tm = min(b * l, 512)
    return dict(nb=proj[0], tl=proj[1], nbd=delta[0], c=delta[1], tm=tm)


def _layer(x, s0, qkv0, conf0, weights):
    wqkv, wrest, wbd, gpre, cqkv, decay, cdw, cvec, wout, wup, wdn, vec = weights
    b, l, _ = x.shape
    t = _tiles(b, l)
    nb = t["nb"]
    rows_first = lambda a: jnp.swapaxes(a, 0, 1).reshape(a.shape[1], b // nb, nb, a.shape[2])
    seq_first = lambda a: jnp.swapaxes(a.reshape(a.shape[0], b, a.shape[3]), 0, 1)
    q, k, v, bg, zs, ga, obg, qtail, ctail = _proj_call(
        x, rows_first(qkv0), rows_first(conf0), wqkv, wrest, wbd, gpre, cqkv, decay, cdw, cvec,
        nb=nb, tl=t["tl"])
    o, s_new = _delta_call(q, k, v, bg, s0, nb=t["nbd"], c=t["c"])
    flat = lambda a: a.reshape(b * l, D_MODEL)
    y = _out_call(flat(x), flat(o), flat(zs), flat(ga), flat(obg), wout, wup, wdn, vec, tm=t["tm"])
    return y.reshape(b, l, D_MODEL), s_new, seq_first(qtail), seq_first(ctail)


_QKV_W = 3 * D_MODEL
_BD_W = 2 * N_HEADS
WT_BLOCK = 512


def _transpose_cast_kernel(wt_ref, out_ref):
    out_ref[...] = wt_ref[...].T.astype(BF16)


def _transpose_cast(wt, row0, n_cols):
    k = wt.shape[1]
    if row0 % WT_BLOCK == 0:
        in_spec = pl.BlockSpec((WT_BLOCK, k), lambda j: (row0 // WT_BLOCK + j, 0))
    else:
        in_spec = pl.BlockSpec((pl.Element(WT_BLOCK), pl.Element(k)),
                               lambda j: ((row0 // 8 + j * (WT_BLOCK // 8)) * 8, 0))
    return pl.pallas_call(
        _transpose_cast_kernel,
        grid=(n_cols // WT_BLOCK,),
        in_specs=[in_spec],
        out_specs=pl.BlockSpec((k, WT_BLOCK), lambda j: (0, j)),
        out_shape=jax.ShapeDtypeStruct((k, n_cols), BF16),
        compiler_params=pltpu.CompilerParams(
            dimension_semantics=("arbitrary",), vmem_limit_bytes=VMEM_LIMIT_BYTES),
        name="transpose_cast",
    )(wt)


def _split_w_in(w_in):
    wt = jnp.swapaxes(w_in, 0, 1)
    rest0 = _QKV_W + _BD_W
    wqkv = _transpose_cast(wt, 0, _QKV_W)
    wrest = _transpose_cast(wt, rest0, wt.shape[0] - rest0)
    wbd = _transpose_cast(wt, _QKV_W, WT_BLOCK)
    return wqkv, wrest, wbd


def _prep_weights(w_in, conv_qkv_w, a_log, dt_bias, delta_norm_g, conf_dw_w, conf_dw_b, conf_ln_g,
                  conf_ln_b, w_out, g_pre_mix, g_post_mix, g_pre_ffn, g_post_ffn, w_up, w_down):
    wqkv, wrest, wbd = _split_w_in(w_in)
    pad_heads = lambda t: jnp.pad(t, (N_HEADS, 128 - 2 * N_HEADS))
    decay = jnp.stack([pad_heads(a_log), pad_heads(dt_bias)]).astype(F32)
    cvec = jnp.stack([conf_dw_b, conf_ln_g, conf_ln_b])
    vec = jnp.stack([jnp.tile(delta_norm_g, N_HEADS), g_post_mix, g_pre_ffn, g_post_ffn])
    return (wqkv, wrest, wbd, g_pre_mix[None, :], conv_qkv_w, decay, conf_dw_w, cvec,
            w_out.astype(BF16), w_up.astype(BF16), w_down.astype(BF16), vec)


def kernel(x_prompt, x_sample, state_delta, state_qkv_conv, state_conf_conv, w_in, conv_qkv_w, a_log,
           dt_bias, delta_norm_g, conf_dw_w, conf_dw_b, conf_ln_g, conf_ln_b, w_out, g_pre_mix,
           g_post_mix, g_pre_ffn, g_post_ffn, w_up, w_down):
    depth = w_in.shape[0]
    bp = x_prompt.shape[0]
    xp, xs = x_prompt, x_sample
    outs = [[] for _ in range(6)]
    for l in range(depth):
        weights = _prep_weights(
            w_in[l], conv_qkv_w[l], a_log[l], dt_bias[l], delta_norm_g[l], conf_dw_w[l], conf_dw_b[l],
            conf_ln_g[l], conf_ln_b[l], w_out[l], g_pre_mix[l], g_post_mix[l], g_pre_ffn[l],
            g_post_ffn[l], w_up[l], w_down[l])
        xp, s_p, bq_p, bc_p = _layer(
            xp, jnp.zeros((bp, N_HEADS, HEAD, HEAD), state_delta.dtype),
            jnp.zeros((bp, SHORT_CONV - 1, 3 * D_MODEL), F32),
            jnp.zeros((bp, CONF_KERNEL - 1, D_MODEL), F32), weights)
        xs, s_s, bq_s, bc_s = _layer(
            xs, state_delta[l], state_qkv_conv[l], state_conf_conv[l], weights)
        for lst, val in zip(outs, (s_p, bq_p, bc_p, s_s, bq_s, bc_s)):
            lst.append(val)
    return (xp, xs) + tuple(jnp.stack(o) for o in outs)
```

```python
import functools

import jax
import jax.numpy as jnp
from jax import lax
from jax.experimental import pallas as pl
from jax.experimental.pallas import tpu as pltpu

D_MODEL = 1024
N_HEADS = 8
HEAD = 128
SHORT_CONV = 4
CONF_KERNEL = 31
D_FF = 4 * D_MODEL
EPS = 1e-6
CHUNK = 64

LANES = 128
SUBLANES = 8
QKV_TAIL = SUBLANES
CONF_TAIL = 4 * SUBLANES
BG_WIDTH = LANES

_C_Q, _C_K, _C_V = 0, D_MODEL, 2 * D_MODEL
_C_Z, _C_GLU_A, _C_GLU_B, _C_GATE_A, _C_GATE_B = (i * D_MODEL for i in range(5))

VMEM_LIMIT_BYTES = 56 * 1024 * 1024

F32 = jnp.float32
BF16 = jnp.bfloat16
HI = lax.Precision.HIGHEST

_NN = (((1,), (0,)), ((), ()))
_NT = (((1,), (1,)), ((), ()))
_TN = (((0,), (0,)), ((), ()))


def _mm(a, b, dims=_NN, exact=False):
    if exact:
        return lax.dot_general(a, b, dims, precision=HI, preferred_element_type=F32)
    return lax.dot_general(a.astype(BF16), b.astype(BF16), dims, preferred_element_type=F32)


def _sigmoid(x):
    return 1.0 / (1.0 + jnp.exp(-x))


def _rms(x, g):
    return x * lax.rsqrt(jnp.mean(x * x, axis=-1, keepdims=True) + EPS) * g


COL_BLOCK = 256


def _tap_groups(tail, taps):
    base = tail - (taps - 1)
    groups = [[o for o in range(base, tail + 1) if o % SUBLANES == res] for res in range(SUBLANES)]
    return base, [g for g in groups if g]


def _tap_group_sum(ext_ref, w_ref, cols, offs, base, tl):
    res = offs[0] % SUBLANES
    rows = tl + SUBLANES if res else tl
    part = None
    for o in offs:
        term = ext_ref[:, o - res:o - res + rows, cols] * w_ref[o - base:o - base + 1, cols]
        part = term if part is None else part + term
    if res:
        nb, _, width = part.shape
        sub = lax.broadcasted_iota(jnp.int32, (nb, tl, width), 1) % SUBLANES
        merged = jnp.where(sub >= res, part[:, :tl, :], part[:, SUBLANES:, :])
        merged = merged.reshape(nb, tl // SUBLANES, SUBLANES, width)
        part = pltpu.roll(merged, SUBLANES - res, axis=2).reshape(nb, tl, width)
    return part


def _emit_interleaved(mxu_jobs, vpu_jobs, lead):
    done = 0
    n_m, n_v = len(mxu_jobs), len(vpu_jobs)
    for i, (need, job) in enumerate(vpu_jobs):
        target = min(n_m, max(need, lead + (i * n_m) // n_v))
        while done < target:
            mxu_jobs[done]()
            done += 1
        job()
    while done < n_m:
        mxu_jobs[done]()
        done += 1


def _proj_kernel(x_ref, qkv0_ref, conf0_ref, wqkv_ref, wrest_ref, wbd_ref, gpre_ref, cqkv_ref,
                 decay_ref, cdw_ref, cvec_ref,
                 q_ref, k_ref, v_ref, bg_ref, zs_ref, ga_ref, obg_ref, qtail_ref, ctail_ref,
                 extq, extc, *, nb, tl):
    r = nb * tl
    nq = SHORT_CONV - 1
    nc = CONF_KERNEL - 1
    n_cb = D_MODEL // COL_BLOCK
    blocks = range(n_cb)
    cb_cols = lambda cb: slice(cb * COL_BLOCK, (cb + 1) * COL_BLOCK)

    @pl.when(pl.program_id(1) == 0)
    def _():
        extq[:, 0:QKV_TAIL, :] = jnp.zeros((nb, QKV_TAIL, 3 * D_MODEL), F32)
        for j in range(nq):
            extq[:, QKV_TAIL - nq + j, :] = qkv0_ref[j, 0]
        extc[:, 0:SUBLANES, :] = jnp.zeros((nb, SUBLANES, D_MODEL), F32)
        for j in range(nc):
            extc[:, CONF_TAIL - nc + j, :] = conf0_ref[j, 0]

    x = x_ref[...].reshape(r, D_MODEL)
    hb = _rms(x, gpre_ref[...]).astype(BF16)
    raw = {}
    proj_jobs = {}

    def add_proj(name, w_ref, c0, width=COL_BLOCK, n_blocks=n_cb, into_extq=False):
        for cb in range(n_blocks):
            def job(cb=cb):
                lo = c0 + cb * width
                val = jnp.dot(hb, w_ref[:, lo:lo + width], preferred_element_type=F32)
                if into_extq:
                    extq[:, QKV_TAIL:QKV_TAIL + tl, lo:lo + width] = val.reshape(nb, tl, width)
                else:
                    raw[name, cb] = val
            proj_jobs[name, cb] = job

    for name, c0 in (("q", _C_Q), ("k", _C_K), ("v", _C_V)):
        add_proj(name, wqkv_ref, c0, into_extq=True)
    for name, c0 in (("glu_a", _C_GLU_A), ("glu_b", _C_GLU_B), ("z", _C_Z), ("gate_b", _C_GATE_B),
                     ("gate_a", _C_GATE_A)):
        add_proj(name, wrest_ref, c0)
    add_proj("bd", wbd_ref, 0, width=BG_WIDTH, n_blocks=1)
    mxu_order = ([(n, cb) for cb in blocks for n in ("glu_a", "glu_b")]
                 + [(n, cb) for n in ("q", "k", "v", "z", "gate_a", "gate_b") for cb in blocks]
                 + [("bd", 0)])
    mxu_jobs = [proj_jobs[key] for key in mxu_order]
    mxu_index = {key: i + 1 for i, key in enumerate(mxu_order)}

    def vpu(needs, job):
        return (max(mxu_index[n] for n in needs) if needs else 0, job)

    qbase, qgroups = _tap_groups(QKV_TAIL, SHORT_CONV)
    qkv_jobs = []
    for name, c0, out_ref in (("q", _C_Q, q_ref), ("k", _C_K, k_ref), ("v", _C_V, v_ref)):
        for cb in blocks:
            def job(c0=c0, out_ref=out_ref, cb=cb):
                cols = slice(c0 + cb * COL_BLOCK, c0 + (cb + 1) * COL_BLOCK)
                a = None
                for offs in qgroups:
                    part = _tap_group_sum(extq, cqkv_ref, cols, offs, qbase, tl)
                    a = part if a is None else a + part
                a = a.reshape(r, COL_BLOCK)
                a = a * _sigmoid(a)
                if out_ref is not v_ref:
                    scale = HEAD ** -0.5 if out_ref is q_ref else 1.0
                    heads = []
                    for h in range(COL_BLOCK // HEAD):
                        ah = a[:, h * HEAD:(h + 1) * HEAD]
                        ah = ah * lax.rsqrt(jnp.sum(ah * ah, axis=-1, keepdims=True) + EPS)
                        heads.append(ah * scale if scale != 1.0 else ah)
                    a = jnp.concatenate(heads, axis=-1)
                out_ref[:, :, cb_cols(cb)] = a.reshape(nb, tl, COL_BLOCK)
            qkv_jobs.append(vpu([(name, cb)], job))

    glu_jobs = []
    for cb in blocks:
        def job(cb=cb):
            glu = raw["glu_a", cb] * _sigmoid(raw["glu_b", cb])
            extc[:, CONF_TAIL:CONF_TAIL + tl, cb_cols(cb)] = glu.reshape(nb, tl, COL_BLOCK)
        glu_jobs.append(vpu([("glu_a", cb), ("glu_b", cb)], job))
    cbase, cgroups = _tap_groups(CONF_TAIL, CONF_KERNEL)
    conv = [None] * n_cb
    conv_jobs = []
    for cb in blocks:
        for offs in cgroups:
            def job(cb=cb, offs=offs):
                part = _tap_group_sum(extc, cdw_ref, cb_cols(cb), offs, cbase, tl)
                conv[cb] = part if conv[cb] is None else conv[cb] + part
            conv_jobs.append(vpu([], job))

    def gate_job(name, out_ref, silu, cb):
        def job():
            t = raw[name, cb]
            t = t * _sigmoid(t) if silu else _sigmoid(t)
            out_ref[:, :, cb_cols(cb)] = t.reshape(nb, tl, COL_BLOCK)
        return vpu([(name, cb)], job)

    def beta_decay():
        lane = lax.broadcasted_iota(jnp.int32, (r, BG_WIDTH), 1)
        bd = jnp.where(lane < 2 * N_HEADS, raw["bd", 0], 0.0)
        xg = bd + decay_ref[1:2, :]
        softplus = jnp.maximum(xg, 0.0) + jnp.log1p(jnp.exp(-jnp.abs(xg)))
        g = -jnp.exp(decay_ref[0:1, :]) * softplus
        bg_ref[...] = jnp.where(lane < N_HEADS, _sigmoid(bd), g).reshape(nb, tl, BG_WIDTH)

    side_jobs = [gate_job(name, out_ref, silu, cb)
                 for name, out_ref, silu in (("z", zs_ref, True), ("gate_a", ga_ref, False))
                 for cb in blocks] + [vpu([("bd", 0)], beta_decay)]

    stats = {}

    def ln_stats():
        cs = [conv[cb].reshape(r, COL_BLOCK) + cvec_ref[0:1, cb_cols(cb)] for cb in blocks]
        mu = sum(jnp.sum(c, axis=-1, keepdims=True) for c in cs) * (1.0 / D_MODEL)
        xc = [c - mu for c in cs]
        var = sum(jnp.sum(v * v, axis=-1, keepdims=True) for v in xc) * (1.0 / D_MODEL)
        stats["xc"] = xc
        stats["rstd"] = lax.rsqrt(var + EPS)
    final_jobs = [vpu([], ln_stats)]
    for cb in blocks:
        def job(cb=cb):
            cols = cb_cols(cb)
            y = stats["xc"][cb] * stats["rstd"] * cvec_ref[1:2, cols] + cvec_ref[2:3, cols]
            obg = _sigmoid(raw["gate_b", cb]) * (y * _sigmoid(y))
            obg_ref[:, :, cols] = obg.reshape(nb, tl, COL_BLOCK)
        final_jobs.append(vpu([("gate_b", cb)], job))

    vpu_jobs = [glu_jobs[0]]
    per_block = len(conv_jobs) // n_cb
    n_qkv = len(qkv_jobs)
    for i, cj in enumerate(conv_jobs):
        if i % per_block == 0 and i // per_block + 1 < n_cb:
            vpu_jobs.append(glu_jobs[i // per_block + 1])
        vpu_jobs.append(cj)
        while qkv_jobs and (n_qkv - len(qkv_jobs)) * len(conv_jobs) < n_qkv * (i + 1):
            vpu_jobs.append(qkv_jobs.pop(0))
    vpu_jobs += side_jobs + final_jobs
    _emit_interleaved(mxu_jobs, vpu_jobs, lead=2)

    for j in range(nq):
        qtail_ref[j, 0] = extq[:, QKV_TAIL + tl - nq + j, :]
    extq[:, 0:QKV_TAIL, :] = extq[:, tl:tl + QKV_TAIL, :]
    for j in range(nc):
        ctail_ref[j, 0] = extc[:, CONF_TAIL + tl - nc + j, :]
    extc[:, 0:CONF_TAIL, :] = extc[:, tl:tl + CONF_TAIL, :]


def _const_spec(shape):
    zeros = (0,) * len(shape)
    return pl.BlockSpec(shape, lambda *_: zeros, pipeline_mode=pl.Buffered(1))


def _proj_call(x, qkv0, conf0, wqkv, wrest, wbd, gpre, cqkv, decay, cdw, cvec, *, nb, tl):
    b, l, _ = x.shape
    grid = (b // nb, l // tl)
    seq = lambda width: pl.BlockSpec((nb, tl, width), lambda i, t: (i, t, 0))
    per_b = lambda rows, width: pl.BlockSpec((rows, 1, nb, width), lambda i, t: (0, i, 0, 0))
    big = jax.ShapeDtypeStruct((b, l, D_MODEL), F32)
    return pl.pallas_call(
        functools.partial(_proj_kernel, nb=nb, tl=tl),
        grid=grid,
        in_specs=[seq(D_MODEL), per_b(SHORT_CONV - 1, 3 * D_MODEL), per_b(CONF_KERNEL - 1, D_MODEL),
                  _const_spec(wqkv.shape), _const_spec(wrest.shape), _const_spec(wbd.shape),
                  _const_spec(gpre.shape), _const_spec(cqkv.shape),
                  _const_spec(decay.shape), _const_spec(cdw.shape), _const_spec(cvec.shape)],
        out_specs=[seq(D_MODEL), seq(D_MODEL), seq(D_MODEL), seq(BG_WIDTH), seq(D_MODEL), seq(D_MODEL),
                   seq(D_MODEL), per_b(SHORT_CONV - 1, 3 * D_MODEL), per_b(CONF_KERNEL - 1, D_MODEL)],
        out_shape=[big, big, big, jax.ShapeDtypeStruct((b, l, BG_WIDTH), F32), big, big, big,
                   jax.ShapeDtypeStruct((SHORT_CONV - 1, b // nb, nb, 3 * D_MODEL), F32),
                   jax.ShapeDtypeStruct((CONF_KERNEL - 1, b // nb, nb, D_MODEL), F32)],
        scratch_shapes=[pltpu.VMEM((nb, QKV_TAIL + tl, 3 * D_MODEL), F32),
                        pltpu.VMEM((nb, CONF_TAIL + tl, D_MODEL), F32)],
        compiler_params=pltpu.CompilerParams(
            dimension_semantics=("arbitrary", "arbitrary"), vmem_limit_bytes=VMEM_LIMIT_BYTES),
        name="proj",
    )(x, qkv0, conf0, wqkv, wrest, wbd, gpre, cqkv, decay, cdw, cvec)


def _delta_kernel(q_ref, k_ref, v_ref, bg_ref, s0_ref, o_ref, sout_ref, s_scr, *, nb, c):
    ci = pl.program_id(1)
    n_pairs = N_HEADS // 2
    pw = 2 * HEAD

    @pl.when(ci == 0)
    def _():
        for bi in range(nb):
            for p in range(n_pairs):
                s_scr[bi, p] = jnp.concatenate([s0_ref[bi, 2 * p], s0_ref[bi, 2 * p + 1]], axis=1)

    iota = lambda shape, d: lax.broadcasted_iota(jnp.int32, shape, d)
    row = iota((c, 2 * c), 0)
    lane = iota((c, 2 * c), 1)
    col = lane % c
    first = lane < c
    causal = row >= col
    strict = row > col
    ident = (row == col).astype(F32)
    first_w = iota((c, pw), 1) < HEAD
    tri = (iota((c, c), 0) >= iota((c, c), 1)).astype(F32)
    sel = (iota((n_pairs, BG_WIDTH), 1) // 2 == iota((n_pairs, BG_WIDTH), 0) + N_HEADS // 2).astype(F32)
    odd_lane = iota((c, BG_WIDTH), 1) % 2 == 1

    def bdiag(x, split):
        return jnp.concatenate([jnp.where(split, x, 0.0), jnp.where(split, 0.0, x)], axis=0)

    chains = [(bi, p) for bi in range(nb) for p in range(n_pairs)]
    each = lambda f, *lists: [f(*args) for args in zip(*lists)]
    psl = lambda p: slice(p * pw, (p + 1) * pw)
    pick = lambda cols, p: jnp.where(first, cols[:, 2 * p:2 * p + 1], cols[:, 2 * p + 1:2 * p + 2])
    pick_w = lambda cols, p: jnp.where(first_w[:cols.shape[0]], cols[:, 2 * p:2 * p + 1],
                                       cols[:, 2 * p + 1:2 * p + 2])

    bgs = [bg_ref[bi] for bi in range(nb)]
    gams = [_mm(tri, bg, exact=True) for bg in bgs]
    gam_ts = [_mm(sel, jnp.concatenate([jnp.where(odd_lane, 0.0, g), jnp.where(odd_lane, g, 0.0)], axis=0),
                  _NT, exact=True) for g in gams]
    gcols = [g[:, N_HEADS:2 * N_HEADS] for g in gams]
    q = [q_ref[bi, :, psl(p)] for bi, p in chains]
    k = [k_ref[bi, :, psl(p)] for bi, p in chains]
    v = [v_ref[bi, :, psl(p)] for bi, p in chains]
    beta = [pick(bgs[bi], p) for bi, p in chains]
    beta_w = [pick_w(bgs[bi], p) for bi, p in chains]
    gc = [pick(gcols[bi], p) for bi, p in chains]
    gc_w = [pick_w(gcols[bi], p) for bi, p in chains]
    gl_w = [pick_w(gcols[bi][c - 1:c, :], p) for bi, p in chains]
    gr = [gam_ts[bi][p:p + 1, :] for bi, p in chains]
    decay = each(lambda a, b: jnp.where(causal, jnp.exp(jnp.where(causal, a - b, 0.0)), 0.0), gc, gr)
    kb = each(lambda a: a.astype(BF16), k)
    kkqk = each(lambda a, b: _mm(jnp.concatenate([a, b.astype(BF16)], axis=0), bdiag(a, first_w), _NT),
                kb, q)
    n = each(lambda b, a, d: jnp.where(strict, -(b * a[:c] * d), 0.0), beta, kkqk, decay)
    eg_w = each(jnp.exp, gc_w)
    t = each(lambda a: ident + a, n)
    pw_ = each(lambda a: _mm(a, bdiag(a, first)), n)
    span = 4
    while span <= c:
        pbd = each(lambda a: bdiag(a, first), pw_)
        if span < c:
            both = each(lambda a, b, m: _mm(jnp.concatenate([a, b], axis=0), m), t, pw_, pbd)
            t = each(lambda a, r: a + r[:c], t, both)
            pw_ = each(lambda r: r[c:], both)
        else:
            t = each(lambda a, m: a + _mm(a, m), t, pbd)
        span *= 2
    rhs = each(lambda b, vv, e, kk_: jnp.concatenate(
        [jnp.concatenate([b[:, :HEAD] * vv[:, :HEAD], (b * e * kk_)[:, :HEAD]], axis=1),
         jnp.concatenate([b[:, HEAD:] * vv[:, HEAD:], (b * e * kk_)[:, HEAD:]], axis=1)], axis=0),
        beta_w, v, eg_w, k)
    sol = each(lambda a, r: _mm(bdiag(a, first), r), t, rhs)
    s = [s_scr[bi, p] for bi, p in chains]
    first_s = iota((HEAD, pw), 1) < HEAD
    sbd = each(lambda a: bdiag(a, first_s).astype(BF16), s)
    kd = each(lambda x: jnp.concatenate([x[:c, HEAD:], x[c:, HEAD:]], axis=1), sol)
    vw = each(lambda x: jnp.concatenate([x[:c, :HEAD], x[c:, :HEAD]], axis=1), sol)
    ks_qs = each(lambda a, qq, e, m: _mm(jnp.concatenate([a, qq * e], axis=0), m), kd, q, eg_w, sbd)
    u = each(lambda a, r: a - r[:c], vw, ks_qs)
    ubd = each(lambda a: bdiag(a, first_w).astype(BF16), u)
    o = each(lambda r, a, d, m: r[c:] + _mm(a[c:] * d, m), ks_qs, kkqk, decay, ubd)
    ko = each(lambda kk_, gl, g: kk_ * jnp.exp(gl - g), k, gl_w, gc_w)
    s_new = each(lambda gl, a, kk_, m: jnp.exp(gl) * a + _mm(
        jnp.concatenate([kk_[:, :HEAD], kk_[:, HEAD:]], axis=0), m, _TN), gl_w, s, ko, ubd)
    for (bi, p), o_p, s_p in zip(chains, o, s_new):
        o_ref[bi, :, psl(p)] = o_p
        s_scr[bi, p] = s_p

    @pl.when(ci == pl.num_programs(1) - 1)
    def _():
        for bi in range(nb):
            for p in range(n_pairs):
                sout_ref[bi, 2 * p] = s_scr[bi, p, :, 0:HEAD]
                sout_ref[bi, 2 * p + 1] = s_scr[bi, p, :, HEAD:pw]


def _delta_call(q, k, v, bg, s0, *, nb, c):
    b, l, _ = q.shape
    seq = lambda width: pl.BlockSpec((nb, c, width), lambda i, t: (i, t, 0))
    st = pl.BlockSpec((nb, N_HEADS, HEAD, HEAD), lambda i, t: (i, 0, 0, 0))
    return pl.pallas_call(
        functools.partial(_delta_kernel, nb=nb, c=c),
        grid=(b // nb, l // c),
        in_specs=[seq(D_MODEL), seq(D_MODEL), seq(D_MODEL), seq(BG_WIDTH), st],
        out_specs=[seq(D_MODEL), st],
        out_shape=[jax.ShapeDtypeStruct((b, l, D_MODEL), F32),
                   jax.ShapeDtypeStruct((b, N_HEADS, HEAD, HEAD), F32)],
        scratch_shapes=[pltpu.VMEM((nb, N_HEADS // 2, HEAD, 2 * HEAD), F32)],
        compiler_params=pltpu.CompilerParams(
            dimension_semantics=("arbitrary", "arbitrary"), vmem_limit_bytes=VMEM_LIMIT_BYTES),
        name="delta",
    )(q, k, v, bg, s0)


FF_SPLIT = 4


def _out_kernel(x_ref, o_ref, zs_ref, ga_ref, obg_ref, wout_ref, wup_ref, wdn_ref, vec_ref, y_ref):
    o = o_ref[...]
    parts = []
    for h in range(N_HEADS):
        oh = o[:, h * HEAD:(h + 1) * HEAD]
        parts.append(oh * lax.rsqrt(jnp.mean(oh * oh, axis=-1, keepdims=True) + EPS))
    oa = jnp.concatenate(parts, axis=-1) * vec_ref[0:1, :] * zs_ref[...]
    mixed = ga_ref[...] * oa + obg_ref[...]
    a = jnp.dot(mixed.astype(BF16), wout_ref[...], preferred_element_type=F32)
    x1 = x_ref[...] + _rms(a, vec_ref[1:2, :])
    hn = _rms(x1, vec_ref[2:3, :]).astype(BF16)
    fw = D_FF // FF_SPLIT
    acc = None
    for i in range(FF_SPLIT):
        hf = jnp.dot(hn, wup_ref[:, i * fw:(i + 1) * fw], preferred_element_type=F32)
        f = jnp.square(jnp.maximum(hf, 0.0)).astype(BF16)
        d = jnp.dot(f, wdn_ref[i * fw:(i + 1) * fw, :], preferred_element_type=F32)
        acc = d if acc is None else acc + d
    y_ref[...] = x1 + _rms(acc, vec_ref[3:4, :])


def _out_call(x, o, zs, ga, obg, wout, wup, wdn, vec, *, tm):
    n = x.shape[0]
    tok = pl.BlockSpec((tm, D_MODEL), lambda i: (i, 0))
    return pl.pallas_call(
        _out_kernel,
        grid=(n // tm,),
        in_specs=[tok, tok, tok, tok, tok, _const_spec(wout.shape), _const_spec(wup.shape),
                  _const_spec(wdn.shape), _const_spec(vec.shape)],
        out_specs=tok,
        out_shape=jax.ShapeDtypeStruct((n, D_MODEL), F32),
        compiler_params=pltpu.CompilerParams(
            dimension_semantics=("arbitrary",), vmem_limit_bytes=VMEM_LIMIT_BYTES),
        name="out",
    )(x, o, zs, ga, obg, wout, wup, wdn, vec)


PROJ_ROWS_SHORT = 128
DELTA_SEQS = 8
OUT_ROWS = 512


def _tiles(b, l):
    if l >= 256:
        proj = (1, 256)
    else:
        proj = (min(b, PROJ_ROWS_SHORT // l), l)
    delta = (min(b, DELTA_SEQS), min(l, CHUNK))
    tm = min(b * l, OUT_ROWS)
    return dict(nb=proj[0], tl=proj[1], nbd=delta[0], c=delta[1], tm=tm)


def _layer(x, s0, qkv0, conf0, weights):
    wqkv, wrest, wbd, gpre, cqkv, decay, cdw, cvec, wout, wup, wdn, vec = weights
    b, l, _ = x.shape
    t = _tiles(b, l)
    nb = t["nb"]
    rows_first = lambda a: jnp.swapaxes(a, 0, 1).reshape(a.shape[1], b // nb, nb, a.shape[2])
    seq_first = lambda a: jnp.swapaxes(a.reshape(a.shape[0], b, a.shape[3]), 0, 1)
    q, k, v, bg, zs, ga, obg, qtail, ctail = _proj_call(
        x, rows_first(qkv0), rows_first(conf0), wqkv, wrest, wbd, gpre, cqkv, decay, cdw, cvec,
        nb=nb, tl=t["tl"])
    o, s_new = _delta_call(q, k, v, bg, s0, nb=t["nbd"], c=t["c"])
    flat = lambda a: a.reshape(b * l, D_MODEL)
    y = _out_call(flat(x), flat(o), flat(zs), flat(ga), flat(obg), wout, wup, wdn, vec, tm=t["tm"])
    return y.reshape(b, l, D_MODEL), s_new, seq_first(qtail), seq_first(ctail)


_QKV_W = 3 * D_MODEL
_BD_W = 2 * N_HEADS
WT_BLOCK = 512


def _transpose_cast_kernel(wt_ref, out_ref):
    out_ref[...] = wt_ref[...].T.astype(BF16)


def _transpose_cast(wt, row0, n_cols):
    k = wt.shape[1]
    if row0 % WT_BLOCK == 0:
        in_spec = pl.BlockSpec((WT_BLOCK, k), lambda j: (row0 // WT_BLOCK + j, 0))
    else:
        in_spec = pl.BlockSpec((pl.Element(WT_BLOCK), pl.Element(k)),
                               lambda j: ((row0 // SUBLANES + j * (WT_BLOCK // SUBLANES)) * SUBLANES, 0))
    return pl.pallas_call(
        _transpose_cast_kernel,
        grid=(n_cols // WT_BLOCK,),
        in_specs=[in_spec],
        out_specs=pl.BlockSpec((k, WT_BLOCK), lambda j: (0, j)),
        out_shape=jax.ShapeDtypeStruct((k, n_cols), BF16),
        compiler_params=pltpu.CompilerParams(
            dimension_semantics=("arbitrary",), vmem_limit_bytes=VMEM_LIMIT_BYTES),
        name="transpose_cast",
    )(wt)


def _split_w_in(w_in):
    wt = jnp.swapaxes(w_in, 0, 1)
    rest0 = _QKV_W + _BD_W
    wqkv = _transpose_cast(wt, 0, _QKV_W)
    wrest = _transpose_cast(wt, rest0, wt.shape[0] - rest0)
    wbd = _transpose_cast(wt, _QKV_W, WT_BLOCK)
    return wqkv, wrest, wbd


def _prep_weights(w_in, conv_qkv_w, a_log, dt_bias, delta_norm_g, conf_dw_w, conf_dw_b, conf_ln_g,
                  conf_ln_b, w_out, g_pre_mix, g_post_mix, g_pre_ffn, g_post_ffn, w_up, w_down):
    wqkv, wrest, wbd = _split_w_in(w_in)
    pad_heads = lambda t: jnp.pad(t, (N_HEADS, BG_WIDTH - 2 * N_HEADS))
    decay = jnp.stack([pad_heads(a_log), pad_heads(dt_bias)]).astype(F32)
    cvec = jnp.stack([conf_dw_b, conf_ln_g, conf_ln_b])
    vec = jnp.stack([jnp.tile(delta_norm_g, N_HEADS), g_post_mix, g_pre_ffn, g_post_ffn])
    return (wqkv, wrest, wbd, g_pre_mix[None, :], conv_qkv_w, decay, conf_dw_w, cvec,
            w_out.astype(BF16), w_up.astype(BF16), w_down.astype(BF16), vec)


def kernel(x_prompt, x_sample, state_delta, state_qkv_conv, state_conf_conv, w_in, conv_qkv_w, a_log,
           dt_bias, delta_norm_g, conf_dw_w, conf_dw_b, conf_ln_g, conf_ln_b, w_out, g_pre_mix,
           g_post_mix, g_pre_ffn, g_post_ffn, w_up, w_down):
    depth = w_in.shape[0]
    bp = x_prompt.shape[0]
    xp, xs = x_prompt, x_sample
    outs = [[] for _ in range(6)]
    for l in range(depth):
        weights = _prep_weights(
            w_in[l], conv_qkv_w[l], a_log[l], dt_bias[l], delta_norm_g[l], conf_dw_w[l], conf_dw_b[l],
            conf_ln_g[l], conf_ln_b[l], w_out[l], g_pre_mix[l], g_post_mix[l], g_pre_ffn[l],
            g_post_ffn[l], w_up[l], w_down[l])
        xp, s_p, bq_p, bc_p = _layer(
            xp, jnp.zeros((bp, N_HEADS, HEAD, HEAD), state_delta.dtype),
            jnp.zeros((bp, SHORT_CONV - 1, 3 * D_MODEL), F32),
            jnp.zeros((bp, CONF_KERNEL - 1, D_MODEL), F32), weights)
        xs, s_s, bq_s, bc_s = _layer(
            xs, state_delta[l], state_qkv_conv[l], state_conf_conv[l], weights)
        for lst, val in zip(outs, (s_p, bq_p, bc_p, s_s, bq_s, bc_s)):
            lst.append(val)
    return (xp, xs) + tuple(jnp.stack(o) for o in outs)
```
